```python
import math
import numpy as np
import jax
import jax.numpy as jnp
from jax import lax

D_MODEL = 1024
BATCH = 32
SEQ = 256
DEPTH = 4
DEC_BATCH = 4
DEC_SEQ = 1024
PAST_LEN = 512

GRID_W = 64
A_HEADS = 4
A_DQK = 64
A_DV = 2 * A_DQK
ROPE_BASE = 10000.0
Q_BLOCK = 128
C_CONV = 512
CONV_W = 31
G_HEADS = 4
G_DK = 128
G_DV = 128
G_SHORT = 3
G_CHUNK = 64
P_HEADS = 8
P_DQ = 256
N_KEYS = 128
N_EXPERTS = N_KEYS * N_KEYS
P_TOPK = 16
P_BLOCK = 128
DN_ALPHA = (2 * DEPTH) ** 0.25
DN_BETA = (8 * DEPTH) ** -0.25
LN_EPS = 1e-5
IN_SIZES = (A_HEADS * 2 * A_DQK, A_HEADS * 2 * A_DQK, A_HEADS * A_DV, 2 * C_CONV,
            G_HEADS * (2 * G_DK + G_DV), G_HEADS * G_DV, 4 * G_HEADS, 3 * D_MODEL)
N_IN = sum(IN_SIZES)

kernel_name = 'hybrid_diffusion_diffattn_conformer_gdn_peer_step'


def layer_norm(x, g, b):
    xf = x.astype(jnp.float32)
    mu = jnp.mean(xf, axis=-1, keepdims=True)
    var = jnp.mean(jnp.square(xf - mu), axis=-1, keepdims=True)
    y = (xf - mu) * lax.rsqrt(var + LN_EPS)
    return (y * g.astype(jnp.float32) + b.astype(jnp.float32)).astype(x.dtype)


def rms_norm(x, g):
    xf = x.astype(jnp.float32)
    y = xf * lax.rsqrt(jnp.mean(jnp.square(xf), axis=-1, keepdims=True) + LN_EPS)
    return (y * g.astype(jnp.float32)).astype(x.dtype)


def l2_normalize(x):
    xf = x.astype(jnp.float32)
    return xf * lax.rsqrt(jnp.sum(jnp.square(xf), axis=-1, keepdims=True) + 1e-6)


def depthwise_conv(x, w, b=None):
    y = lax.conv_general_dilated(x, w.astype(x.dtype)[:, None, :], window_strides=(1,), padding='SAME',
                                 dimension_numbers=('NWC', 'WIO', 'NWC'), feature_group_count=x.shape[-1])
    return y if b is None else y + b.astype(x.dtype)


def axial_angles(n_tokens):
    n_rows = n_tokens // GRID_W
    rows = jnp.repeat(jnp.arange(n_rows, dtype=jnp.float32), GRID_W)
    cols = jnp.tile(jnp.arange(GRID_W, dtype=jnp.float32), n_rows)
    half = A_DQK // 2
    inv_freq = 1.0 / (ROPE_BASE ** (jnp.arange(0, half, 2, dtype=jnp.float32) / half))
    return rows[:, None] * inv_freq, cols[:, None] * inv_freq


def rotate(x, ang):
    x1, x2 = jnp.split(x, 2, axis=-1)
    cos = jnp.cos(ang).astype(x.dtype)
    sin = jnp.sin(ang).astype(x.dtype)
    return jnp.concatenate([x1 * cos - x2 * sin, x1 * sin + x2 * cos], axis=-1)


def axial_rope(x, ang_r, ang_c):
    half = A_DQK // 2
    ar = ang_r[:, None, None, :]
    ac = ang_c[:, None, None, :]
    return jnp.concatenate([rotate(x[..., :half], ar), rotate(x[..., half:], ac)], axis=-1)


def diff_attention(q, k, v, lam):
    b, tq = q.shape[0], q.shape[1]
    nb = tq // Q_BLOCK
    qb = q.reshape(b, nb, Q_BLOCK, A_HEADS, 2, A_DQK).transpose(1, 0, 2, 3, 4, 5)
    scale = A_DQK ** -0.5

    def one_block(qblk):
        s = jnp.einsum('bqhmd,bkhmd->bhmqk', qblk, k).astype(jnp.float32) * scale
        p = jax.nn.softmax(s, axis=-1)
        a = p[:, :, 0] - lam * p[:, :, 1]
        return jnp.einsum('bhqk,bkhe->bqhe', a.astype(v.dtype), v)

    o = lax.map(one_block, qb)
    return o.transpose(1, 0, 2, 3, 4).reshape(b, tq, A_HEADS, A_DV)


def chunk_gated_delta(q, k, v, g, beta, s0):
    f32 = jnp.float32
    b, t, h, dk = q.shape
    dv = v.shape[-1]
    c = G_CHUNK
    n = t // c

    def chunks(a):
        return a.astype(f32).reshape(b, n, c, h, -1).transpose(1, 0, 3, 2, 4)

    qc, kc, vc = chunks(q), chunks(k), chunks(v)
    gc = chunks(g[..., None])[..., 0]
    bc = chunks(beta[..., None])[..., 0]
    gcum = jnp.cumsum(gc, axis=-1)
    causal = jnp.tril(jnp.ones((c, c), dtype=bool))
    strict = jnp.tril(jnp.ones((c, c), dtype=bool), k=-1)
    decay = jnp.exp(jnp.where(causal, gcum[..., :, None] - gcum[..., None, :], -jnp.inf))
    kbeta = kc * bc[..., None]
    lmat = jnp.where(strict, jnp.einsum('nbhid,nbhjd->nbhij', kbeta, kc) * decay, 0.0)
    rhs = jnp.concatenate([kbeta * jnp.exp(gcum)[..., None], vc * bc[..., None]], axis=-1)
    sol = lax.linalg.triangular_solve(jnp.eye(c, dtype=f32) + lmat, rhs, left_side=True, lower=True,
                                      unit_diagonal=True)
    w, u = sol[..., :dk], sol[..., dk:]
    qk = jnp.where(causal, jnp.einsum('nbhid,nbhjd->nbhij', qc, kc) * decay, 0.0)

    def step(state, xs):
        q_i, k_i, u_i, w_i, g_i, qk_i = xs
        v_new = u_i - jnp.einsum('bhck,bhkv->bhcv', w_i, state)
        out = (jnp.einsum('bhck,bhkv->bhcv', q_i * jnp.exp(g_i)[..., None], state)
               + jnp.einsum('bhij,bhjv->bhiv', qk_i, v_new))
        g_last = g_i[..., -1:]
        k_dec = k_i * jnp.exp(g_last - g_i)[..., None]
        state = state * jnp.exp(g_last)[..., None] + jnp.einsum('bhck,bhcv->bhkv', k_dec, v_new)
        return state, out

    s_final, out = lax.scan(step, s0.astype(f32), (qc, kc, u, w, gcum, qk))
    return out.transpose(1, 0, 3, 2, 4).reshape(b, t, h, dv), s_final


def token_mixer(h, l, w_in, diff_lambda, diff_norm_g, w_attn_o, conv_dw_w, conv_dw_b, conv_ln_g, conv_ln_b,
                w_conv_o, gdn_conv_w, gdn_A_log, gdn_dt_bias, gdn_norm_g, w_gdn_o, w_out,
                rope, ctx_k, ctx_v, s0):
    b, t, _ = h.shape
    split_at = np.cumsum(IN_SIZES)[:-1].tolist()
    aq, ak, av, cin, gqkv, gz, gab, mg = jnp.split(h @ w_in, split_at, axis=-1)

    q = aq.reshape(b, t, A_HEADS, 2, A_DQK)
    k = ak.reshape(b, t, A_HEADS, 2, A_DQK)
    v = av.reshape(b, t, A_HEADS, A_DV)
    if rope is None:
        q_pos, k_pos = q, k
    else:
        q_pos, k_pos = axial_rope(q, rope[0], rope[1]), axial_rope(k, rope[0], rope[1])
    if ctx_k is None:
        keys, vals = k_pos, v
    else:
        keys = jnp.concatenate([ctx_k.astype(k_pos.dtype), k_pos], axis=1)
        vals = jnp.concatenate([ctx_v.astype(v.dtype), v], axis=1)
    lam_init = 0.8 - 0.6 * math.exp(-0.3 * l)
    dl = diff_lambda.astype(jnp.float32)
    lam = jnp.exp(jnp.sum(dl[0] * dl[1])) - jnp.exp(jnp.sum(dl[2] * dl[3])) + lam_init
    o_a = rms_norm(diff_attention(q_pos, keys, vals, lam), diff_norm_g) * (1.0 - lam_init)
    attn_out = o_a.reshape(b, t, A_HEADS * A_DV) @ w_attn_o

    ca, cb = jnp.split(cin, 2, axis=-1)
    cg = depthwise_conv(ca * jax.nn.sigmoid(cb), conv_dw_w, conv_dw_b)
    conv_out = jax.nn.silu(layer_norm(cg, conv_ln_g, conv_ln_b)) @ w_conv_o

    qkv = jax.nn.silu(depthwise_conv(gqkv, gdn_conv_w))
    gq, gk, gv = jnp.split(qkv, [G_HEADS * G_DK, 2 * G_HEADS * G_DK], axis=-1)
    gq = l2_normalize(gq.reshape(b, t, G_HEADS, G_DK)) * (G_DK ** -0.5)
    gk = l2_normalize(gk.reshape(b, t, G_HEADS, G_DK))
    gv = gv.reshape(b, t, G_HEADS, G_DV)
    ga = gab[..., :2 * G_HEADS].reshape(b, t, 2, G_HEADS).astype(jnp.float32)
    gb = gab[..., 2 * G_HEADS:].reshape(b, t, 2, G_HEADS).astype(jnp.float32)
    decay = -jnp.exp(gdn_A_log.astype(jnp.float32)) * jax.nn.softplus(ga + gdn_dt_bias.astype(jnp.float32))
    beta = jax.nn.sigmoid(gb)
    o_f, s_f = chunk_gated_delta(gq, gk, gv, decay[:, :, 0], beta[:, :, 0], s0[:, 0])
    rev = lambda a: jnp.flip(a, axis=1)
    o_b, s_b = chunk_gated_delta(rev(gq), rev(gk), rev(gv), rev(decay[:, :, 1]), rev(beta[:, :, 1]), s0[:, 1])
    o_g = (o_f + rev(o_b)).astype(h.dtype)
    o_g = rms_norm(o_g, gdn_norm_g) * jax.nn.silu(gz.reshape(b, t, G_HEADS, G_DV))
    gdn_out = o_g.reshape(b, t, G_HEADS * G_DV) @ w_gdn_o

    gates = jax.nn.sigmoid(mg.reshape(b, t, 3, D_MODEL))
    merged = gates[:, :, 0] * attn_out + gates[:, :, 1] * conv_out + gates[:, :, 2] * gdn_out
    new_k = k.reshape(b, t, A_HEADS, 2 * A_DQK)
    new_s = jnp.stack([s_f, s_b], axis=1)
    return merged @ w_out, new_k, v, new_s


def peer(h, wq, keys, u_tab, v_tab):
    b, t, d = h.shape
    flat = h.reshape(-1, P_BLOCK, d)

    def one_block(hb):
        q = (hb @ wq).reshape(P_BLOCK, P_HEADS, 2, P_DQ // 2)
        s = jnp.einsum('thpd,pkd->thpk', q, keys).astype(jnp.float32)
        sv, si = lax.top_k(s, P_TOPK)
        cand_s = (sv[:, :, 0, :, None] + sv[:, :, 1, None, :]).reshape(P_BLOCK, P_HEADS, P_TOPK * P_TOPK)
        cand_i = (si[:, :, 0, :, None] * N_KEYS + si[:, :, 1, None, :]).reshape(P_BLOCK, P_HEADS, P_TOPK * P_TOPK)
        top_s, pos = lax.top_k(cand_s, P_TOPK)
        expert = jnp.take_along_axis(cand_i, pos, axis=-1)
        gate = jax.nn.softmax(top_s, axis=-1)
        u = jnp.take(u_tab, expert, axis=0)
        act = jax.nn.gelu(jnp.einsum('thkd,td->thk', u, hb).astype(jnp.float32), approximate=False)
        vv = jnp.take(v_tab, expert, axis=0)
        return jnp.einsum('thk,thkd->td', (gate * act).astype(h.dtype), vv)

    return lax.map(one_block, flat).reshape(b, t, d)


def trunk_layer(x, cond, l, w_mod, b_mod, w_in, diff_lambda, diff_norm_g, w_attn_o, conv_dw_w, conv_dw_b,
                conv_ln_g, conv_ln_b, w_conv_o, gdn_conv_w, gdn_A_log, gdn_dt_bias, gdn_norm_g, w_gdn_o, w_out,
                ln_g, ln_b, peer_wq, peer_keys, peer_u, peer_v, rope, ctx_k, ctx_v, s0):
    mod = (jax.nn.silu(cond) @ w_mod + b_mod).reshape(cond.shape[0], 1, 6, D_MODEL)
    sh1, sc1, g1, sh2, sc2, g2 = [mod[:, :, i] for i in range(6)]
    h = x * (1.0 + sc1) + sh1
    mix, new_k, new_v, new_s = token_mixer(h, l, w_in, diff_lambda, diff_norm_g, w_attn_o, conv_dw_w, conv_dw_b,
                                           conv_ln_g, conv_ln_b, w_conv_o, gdn_conv_w, gdn_A_log, gdn_dt_bias,
                                           gdn_norm_g, w_gdn_o, w_out, rope, ctx_k, ctx_v, s0)
    x = layer_norm(DN_ALPHA * x + g1 * mix, ln_g[0], ln_b[0])
    h = x * (1.0 + sc2) + sh2
    x = layer_norm(DN_ALPHA * x + g2 * peer(h, peer_wq, peer_keys, peer_u, peer_v), ln_g[1], ln_b[1])
    return x, new_k, new_v, new_s


def setup_inputs(seed: int = 0) -> dict:
    key = jax.random.key(seed)
    ks = jax.random.split(key, 40)
    f32 = jnp.float32

    def nrm(i, shape, scale=1.0):
        return jax.random.normal(ks[i], shape, f32) * scale

    dt = jnp.exp(jax.random.uniform(ks[30], (DEPTH, 2, G_HEADS), f32, math.log(1e-3), math.log(1e-1)))
    return {
        'x_prompt': nrm(0, (BATCH, SEQ, D_MODEL)),
        'x_sample': nrm(1, (DEC_BATCH, DEC_SEQ, D_MODEL)),
        'cache_attn_k': nrm(2, (DEC_BATCH, DEPTH, PAST_LEN, A_HEADS, 2 * A_DQK)),
        'cache_attn_v': nrm(3, (DEC_BATCH, DEPTH, PAST_LEN, A_HEADS, A_DV)),
        'state_gdn': nrm(4, (DEC_BATCH, DEPTH, 2, G_HEADS, G_DK, G_DV), 0.1),
        'c': nrm(5, (DEC_BATCH, D_MODEL)),
        'c_ctx': nrm(6, (D_MODEL,)),
        'w_mod': nrm(7, (DEPTH, D_MODEL, 6 * D_MODEL), 0.5 * D_MODEL ** -0.5),
        'b_mod': nrm(8, (DEPTH, 6 * D_MODEL), 0.01),
        'w_in': nrm(9, (DEPTH, D_MODEL, N_IN), D_MODEL ** -0.5),
        'diff_lambda': nrm(10, (DEPTH, 4, A_DQK), 0.1),
        'diff_norm_g': 1.0 + nrm(11, (DEPTH, A_DV), 0.01),
        'w_attn_o': nrm(12, (DEPTH, A_HEADS * A_DV, D_MODEL), (A_HEADS * A_DV) ** -0.5),
        'conv_dw_w': nrm(13, (DEPTH, CONV_W, C_CONV), CONV_W ** -0.5),
        'conv_dw_b': nrm(14, (DEPTH, C_CONV), 0.01),
        'conv_ln_g': 1.0 + nrm(15, (DEPTH, C_CONV), 0.01),
        'conv_ln_b': nrm(16, (DEPTH, C_CONV), 0.01),
        'w_conv_o': nrm(17, (DEPTH, C_CONV, D_MODEL), C_CONV ** -0.5),
        'gdn_conv_w': nrm(18, (DEPTH, G_SHORT, G_HEADS * (2 * G_DK + G_DV)), G_SHORT ** -0.5),
        'gdn_A_log': jnp.log(jax.random.uniform(ks[19], (DEPTH, 2, G_HEADS), f32, 1.0, 16.0)),
        'gdn_dt_bias': dt + jnp.log(-jnp.expm1(-dt)),
        'gdn_norm_g': 1.0 + nrm(20, (DEPTH, G_DV), 0.01),
        'w_gdn_o': nrm(21, (DEPTH, G_HEADS * G_DV, D_MODEL), (G_HEADS * G_DV) ** -0.5),
        'w_out': nrm(22, (DEPTH, D_MODEL, D_MODEL), DN_BETA * D_MODEL ** -0.5),
        'ln_g': 1.0 + nrm(23, (DEPTH, 2, D_MODEL), 0.01),
        'ln_b': nrm(24, (DEPTH, 2, D_MODEL), 0.01),
        'peer_wq': nrm(25, (DEPTH, D_MODEL, P_HEADS * P_DQ), D_MODEL ** -0.5),
        'peer_keys': nrm(26, (DEPTH, 2, N_KEYS, P_DQ // 2), (P_DQ // 2) ** -0.5),
        'peer_u': nrm(27, (DEPTH, N_EXPERTS, D_MODEL), D_MODEL ** -0.5),
        'peer_v': nrm(28, (DEPTH, N_EXPERTS, D_MODEL), DN_BETA * P_HEADS ** -0.5),
    }


def reference(x_prompt, x_sample, cache_attn_k, cache_attn_v, state_gdn, c, c_ctx, w_mod, b_mod, w_in,
              diff_lambda, diff_norm_g, w_attn_o, conv_dw_w, conv_dw_b, conv_ln_g, conv_ln_b, w_conv_o,
              gdn_conv_w, gdn_A_log, gdn_dt_bias, gdn_norm_g, w_gdn_o, w_out, ln_g, ln_b,
              peer_wq, peer_keys, peer_u, peer_v):
    b_ctx, b_lat = x_prompt.shape[0], x_sample.shape[0]
    past = cache_attn_k.shape[2]
    rope = axial_angles(x_sample.shape[1])
    cond_ctx = c_ctx[None, :]
    s0_ctx = jnp.zeros((b_ctx, 2, G_HEADS, G_DK, G_DV), jnp.float32)
    xp, xs = x_prompt, x_sample
    k_list, v_list, s_list = [], [], []
    for l in range(DEPTH):
        def layer(x, cond, rp, ctx_k, ctx_v, s0, l=l):
            return trunk_layer(x, cond, l, w_mod[l], b_mod[l], w_in[l], diff_lambda[l], diff_norm_g[l],
                               w_attn_o[l], conv_dw_w[l], conv_dw_b[l], conv_ln_g[l], conv_ln_b[l], w_conv_o[l],
                               gdn_conv_w[l], gdn_A_log[l], gdn_dt_bias[l], gdn_norm_g[l], w_gdn_o[l], w_out[l],
                               ln_g[l], ln_b[l], peer_wq[l], peer_keys[l], peer_u[l], peer_v[l],
                               rp, ctx_k, ctx_v, s0)
        xp, k_l, v_l, s_l = layer(xp, cond_ctx, None, None, None, s0_ctx)
        k_list.append(k_l)
        v_list.append(v_l)
        s_list.append(s_l.astype(x_prompt.dtype))
        ctx_k = cache_attn_k[:, l].reshape(b_lat, past, A_HEADS, 2, A_DQK)
        xs, _, _, _ = layer(xs, c, rope, ctx_k, cache_attn_v[:, l], state_gdn[:, l])
    new_attn_k = jnp.stack(k_list, axis=1)
    new_attn_v = jnp.stack(v_list, axis=1)
    new_state_gdn = jnp.stack(s_list, axis=1)
    return (xp, xs, new_attn_k, new_attn_v, new_state_gdn)
```

```python
import functools
import math

import numpy as np
import jax
import jax.numpy as jnp
from jax import lax
from jax.experimental import pallas as pl
from jax.experimental.pallas import tpu as pltpu

F32 = jnp.float32
BF16 = jnp.bfloat16

LANES = 128
SUBLANES = 8
VMEM_LIMIT = 56 * 1024 * 1024

D_MODEL = 1024
A_HEADS = 4
A_DQK = 64
A_DV = 128
GRID_W = 64
ROPE_BASE = 10000.0
C_CONV = 512
CONV_W = 31
G_HEADS = 4
G_DK = 128
G_SHORT = 3
G_CHUNK = 128
P_HEADS = 8
N_KEYS = 128
P_TOPK = 16
LN_EPS = 1e-5

N_PROJ = 8192
OFF_MG, OFF_CA, OFF_CB, OFF_AQ, OFF_AK, OFF_AV, OFF_GZ, OFF_GQ, OFF_GK, OFF_GV, OFF_GAB = (
    0, 3072, 3584, 4096, 4608, 5120, 5632, 6144, 6656, 7168, 7680)

TB_PROJ = 512
TN_PROJ = 2048
TB_MERGE = 256
TB_PEER = 512
EB_PEER = 1024


def _cparams(*sem):
    return pltpu.CompilerParams(dimension_semantics=sem, vmem_limit_bytes=VMEM_LIMIT)


def _mm(a, b):
    return jnp.dot(a.astype(BF16), b.astype(BF16), preferred_element_type=F32)


def _mm_nt(a, b):
    return lax.dot_general(a.astype(BF16), b.astype(BF16), (((1,), (1,)), ((), ())),
                           preferred_element_type=F32)


def _mm_f32(a, b):
    return jnp.dot(a, b, preferred_element_type=F32, precision=lax.Precision.HIGHEST)


def _silu(x):
    return x * jax.nn.sigmoid(x)


def _mod_row(i, n_ctx_blocks, blocks_per_lat):
    return jnp.where(i < n_ctx_blocks, 0, 1 + (i - n_ctx_blocks) // blocks_per_lat)


def _layer_norm(x, g, b):
    mu = jnp.mean(x, axis=-1, keepdims=True)
    xc = x - mu
    var = jnp.mean(xc * xc, axis=-1, keepdims=True)
    return xc * lax.rsqrt(var + LN_EPS) * g + b


def _mod_kernel(cond_ref, w_ref, b_ref, o_ref):
    a = _silu(cond_ref[...])
    o_ref[0] = _mm(a, w_ref[0]) + b_ref[0]


def _mod_call(cond8, w_mod, b_mod):
    depth, d, n = w_mod.shape
    tn = 1536
    return pl.pallas_call(
        _mod_kernel,
        grid=(depth, n // tn),
        in_specs=[pl.BlockSpec((SUBLANES, d), lambda l, j: (0, 0)),
                  pl.BlockSpec((1, d, tn), lambda l, j: (l, 0, j)),
                  pl.BlockSpec((1, 1, tn), lambda l, j: (l, 0, j))],
        out_specs=pl.BlockSpec((1, SUBLANES, tn), lambda l, j: (l, 0, j)),
        out_shape=jax.ShapeDtypeStruct((depth, SUBLANES, n), F32),
        compiler_params=_cparams("arbitrary", "arbitrary"),
    )(cond8, w_mod, b_mod.reshape(depth, 1, n))


def _inproj_kernel(x_ref, mod_ref, w_ref, o_ref, *, n_ctx_blocks, blocks_per_lat):
    row = _mod_row(pl.program_id(1), n_ctx_blocks, blocks_per_lat)
    sh = mod_ref[pl.ds(row, 1), 0:D_MODEL]
    sc = mod_ref[pl.ds(row, 1), D_MODEL:2 * D_MODEL]
    h = x_ref[...] * (1.0 + sc) + sh
    o_ref[...] = _mm(h, w_ref[...])


def _inproj_call(x, mod_l, w_p, n_ctx_tok, lat_seq):
    n_tok = x.shape[0]
    kern = functools.partial(_inproj_kernel, n_ctx_blocks=n_ctx_tok // TB_PROJ,
                             blocks_per_lat=lat_seq // TB_PROJ)
    return pl.pallas_call(
        kern,
        grid=(N_PROJ // TN_PROJ, n_tok // TB_PROJ),
        in_specs=[pl.BlockSpec((TB_PROJ, D_MODEL), lambda j, i: (i, 0)),
                  pl.BlockSpec((SUBLANES, 6 * D_MODEL), lambda j, i: (0, 0)),
                  pl.BlockSpec((D_MODEL, TN_PROJ), lambda j, i: (0, j))],
        out_specs=pl.BlockSpec((TB_PROJ, TN_PROJ), lambda j, i: (i, j)),
        out_shape=jax.ShapeDtypeStruct((n_tok, N_PROJ), F32),
        compiler_params=_cparams("arbitrary", "arbitrary"),
    )(x, mod_l, w_p)


def _softmax(s):
    e = jnp.exp(s - jnp.max(s, axis=-1, keepdims=True))
    return e / jnp.sum(e, axis=-1, keepdims=True)


def _diff_lambda(dl_ref, lam_init):
    dl = dl_ref[...]
    a = jnp.sum(dl[0:1] * dl[1:2], axis=-1, keepdims=True)
    b = jnp.sum(dl[2:3] * dl[3:4], axis=-1, keepdims=True)
    return jnp.exp(a) - jnp.exp(b) + lam_init


def _diff_attn_heads(q, k_of, v_of, lam, ng, lam_init, o_ref):
    lo = lax.broadcasted_iota(jnp.int32, (1, LANES), 1) < A_DQK
    scale = A_DQK ** -0.5
    for h in range(A_HEADS):
        qh = q[:, h * LANES:(h + 1) * LANES]
        kh = k_of(h)
        s1 = _mm_nt(jnp.where(lo, qh, 0.0), kh) * scale
        s2 = _mm_nt(jnp.where(lo, 0.0, qh), kh) * scale
        a = _softmax(s1) - lam * _softmax(s2)
        o = _mm(a, v_of(h))
        o = o * lax.rsqrt(jnp.mean(o * o, axis=-1, keepdims=True) + LN_EPS) * ng
        o_ref[:, h * LANES:(h + 1) * LANES] = (o * (1.0 - lam_init)).astype(o_ref.dtype)


def _attn_ctx_kernel(q_ref, k_ref, v_ref, dl_ref, ng_ref, o_ref, *, lam_init):
    lam = _diff_lambda(dl_ref, lam_init)
    _diff_attn_heads(q_ref[...],
                     lambda h: k_ref[:, h * LANES:(h + 1) * LANES].astype(BF16),
                     lambda h: v_ref[:, h * LANES:(h + 1) * LANES].astype(BF16),
                     lam, ng_ref[...], lam_init, o_ref)


def _attn_ctx_call(proj, dl, ng, n_seq, seq, lam_init):
    w = A_HEADS * LANES
    spec = lambda off: pl.BlockSpec((seq, w), lambda b, off=off: (b, off // w))
    return pl.pallas_call(
        functools.partial(_attn_ctx_kernel, lam_init=lam_init),
        grid=(n_seq,),
        in_specs=[spec(OFF_AQ), spec(OFF_AK), spec(OFF_AV),
                  pl.BlockSpec(dl.shape, lambda b: (0, 0)),
                  pl.BlockSpec((1, LANES), lambda b: (0, 0))],
        out_specs=pl.BlockSpec((seq, w), lambda b: (b, 0)),
        out_shape=jax.ShapeDtypeStruct((n_seq * seq, w), BF16),
        compiler_params=_cparams("arbitrary"),
    )(proj, proj, proj, dl, ng)


def _rope(x, cos, sin_lo, sin_hi):
    n = x.shape[-1]
    return x * cos + pltpu.roll(x, n - 16, 1) * sin_lo + pltpu.roll(x, 16, 1) * sin_hi


def _attn_lat_kernel(q_ref, k_ref, v_ref, ck_ref, cv_ref, cosq_ref, slq_ref, shq_ref,
                     cosk_ref, slk_ref, shk_ref, dl_ref, ng_ref, o_ref, kall_ref, vall_ref,
                     *, lam_init, past):
    @pl.when(pl.program_id(1) == 0)
    def _():
        kall_ref[0:past, :] = ck_ref[0, 0].astype(BF16)
        vall_ref[0:past, :] = cv_ref[0, 0].astype(BF16)
        kall_ref[past:, :] = _rope(k_ref[...], cosk_ref[...], slk_ref[...], shk_ref[...]).astype(BF16)
        vall_ref[past:, :] = v_ref[...].astype(BF16)

    lam = _diff_lambda(dl_ref, lam_init)
    q = _rope(q_ref[...], cosq_ref[...], slq_ref[...], shq_ref[...])
    _diff_attn_heads(q,
                     lambda h: kall_ref[:, h * LANES:(h + 1) * LANES],
                     lambda h: vall_ref[:, h * LANES:(h + 1) * LANES],
                     lam, ng_ref[...], lam_init, o_ref)


def _attn_lat_call(proj, cache_k, cache_v, layer, rope_tabs, dl, ng, n_ctx_tok, n_seq, seq, lam_init):
    w = A_HEADS * LANES
    past = cache_k.shape[2]
    qb = 256
    nqb = seq // qb
    row0 = n_ctx_tok // seq
    rowq0 = n_ctx_tok // qb
    cos, s_lo, s_hi = rope_tabs
    qspec = pl.BlockSpec((qb, w), lambda b, i: (rowq0 + b * nqb + i, OFF_AQ // w))
    kspec = pl.BlockSpec((seq, w), lambda b, i: (row0 + b, OFF_AK // w))
    vspec = pl.BlockSpec((seq, w), lambda b, i: (row0 + b, OFF_AV // w))
    cspec = pl.BlockSpec((1, 1, past, w), lambda b, i: (b, layer, 0, 0))
    tq = pl.BlockSpec((qb, w), lambda b, i: (i, 0))
    tk = pl.BlockSpec((seq, w), lambda b, i: (0, 0))
    return pl.pallas_call(
        functools.partial(_attn_lat_kernel, lam_init=lam_init, past=past),
        grid=(n_seq, nqb),
        in_specs=[qspec, kspec, vspec, cspec, cspec, tq, tq, tq, tk, tk, tk,
                  pl.BlockSpec(dl.shape, lambda b, i: (0, 0)),
                  pl.BlockSpec((1, LANES), lambda b, i: (0, 0))],
        out_specs=pl.BlockSpec((qb, w), lambda b, i: (b * nqb + i, 0)),
        out_shape=jax.ShapeDtypeStruct((n_seq * seq, w), BF16),
        scratch_shapes=[pltpu.VMEM((past + seq, w), BF16), pltpu.VMEM((past + seq, w), BF16)],
        compiler_params=_cparams("arbitrary", "arbitrary"),
    )(proj, proj, proj, cache_k, cache_v, cos, s_lo, s_hi, cos, s_lo, s_hi, dl, ng)


CONV_PAD = 16
CONV_ROWS = 64


def _conv_kernel(ca_ref, cb_ref, w_ref, b_ref, lg_ref, lb_ref, o_ref, pad_ref, *, seq):
    zeros = jnp.zeros((CONV_PAD, C_CONV), F32)
    pad_ref[0:CONV_PAD, :] = zeros
    pad_ref[seq + CONV_PAD:seq + 2 * CONV_PAD, :] = zeros
    pad_ref[CONV_PAD:seq + CONV_PAD, :] = ca_ref[...] * jax.nn.sigmoid(cb_ref[...])
    half = CONV_W // 2

    def body(ci, carry):
        base = pl.multiple_of(ci * CONV_ROWS, CONV_ROWS)
        win = pad_ref[pl.ds(base, CONV_ROWS + 2 * CONV_PAD), :]
        acc = jnp.zeros((CONV_ROWS, C_CONV), F32)
        for r in range(SUBLANES):
            shifted = win[r:r + CONV_ROWS + 3 * SUBLANES]
            for qq in range(4):
                j = qq * SUBLANES + r - (CONV_PAD - half)
                if 0 <= j < CONV_W:
                    acc = acc + w_ref[j:j + 1, :] * shifted[qq * SUBLANES:qq * SUBLANES + CONV_ROWS]
        y = _layer_norm(acc + b_ref[...], lg_ref[...], lb_ref[...])
        o_ref[pl.ds(base, CONV_ROWS), :] = _silu(y).astype(o_ref.dtype)
        return carry

    lax.fori_loop(0, seq // CONV_ROWS, body, 0)


def _conv_call(proj, w, b, lg, lb, row0, n_seq, seq):
    spec = lambda off: pl.BlockSpec((seq, C_CONV), lambda s, off=off: (row0 + s, off // C_CONV))
    vec = pl.BlockSpec((1, C_CONV), lambda s: (0, 0))
    return pl.pallas_call(
        functools.partial(_conv_kernel, seq=seq),
        grid=(n_seq,),
        in_specs=[spec(OFF_CA), spec(OFF_CB), pl.BlockSpec(w.shape, lambda s: (0, 0)), vec, vec, vec],
        out_specs=pl.BlockSpec((seq, C_CONV), lambda s: (s, 0)),
        out_shape=jax.ShapeDtypeStruct((n_seq * seq, C_CONV), BF16),
        scratch_shapes=[pltpu.VMEM((seq + 2 * CONV_PAD, C_CONV), F32)],
        compiler_params=_cparams("arbitrary"),
    )(proj, proj, w, b, lg, lb)


def _unit_tri_inverse(l_mat):
    n = l_mat.shape[0]
    eye = (lax.broadcasted_iota(jnp.int32, (n, n), 0) ==
           lax.broadcasted_iota(jnp.int32, (n, n), 1)).astype(F32)
    p = eye - l_mat
    m = l_mat
    steps = int(math.log2(n)) - 1
    for i in range(steps):
        m = _mm_f32(m, m)
        p = p + _mm_f32(p, m)
    return p


def _gdn_kernel(*refs, seq, has_s0, emit_state):
    (q_ref, k_ref, v_ref, gab_ref, z_ref, cwq_ref, cwk_ref, cwv_ref, al_ref, dt_ref, ng_ref) = refs[:11]
    idx = 11
    s0_ref = None
    if has_s0:
        s0_ref = refs[idx]
        idx += 1
    o_ref = refs[idx]
    idx += 1
    sn_ref = None
    if emit_state:
        sn_ref = refs[idx]
        idx += 1
    (pad_ref, kc_s, kt_s, qc_s, vc_s, gcb_s, btb_s, grow_s, w_s, u_s, qk_s, cumt_ref, oacc_s) = refs[idx:]

    c = G_CHUNK
    n_chunks = seq // c
    head = pl.program_id(1)

    zeros8 = jnp.zeros((SUBLANES, LANES), F32)
    pad_ref[0:SUBLANES, :] = zeros8
    pad_ref[seq + SUBLANES:seq + 2 * SUBLANES, :] = zeros8

    def short_conv(x_ref, w_ref):
        pad_ref[SUBLANES:seq + SUBLANES, :] = x_ref[...]
        y = (w_ref[0:1, :] * pad_ref[SUBLANES - 1:seq + SUBLANES - 1, :]
             + w_ref[1:2, :] * pad_ref[SUBLANES:seq + SUBLANES, :]
             + w_ref[2:3, :] * pad_ref[SUBLANES + 1:seq + SUBLANES + 1, :])
        return _silu(y)

    q = short_conv(q_ref, cwq_ref)
    k = short_conv(k_ref, cwk_ref)
    v = short_conv(v_ref, cwv_ref)
    qn = q * lax.rsqrt(jnp.sum(q * q, axis=-1, keepdims=True) + 1e-6) * (G_DK ** -0.5)
    kn = k * lax.rsqrt(jnp.sum(k * k, axis=-1, keepdims=True) + 1e-6)

    gab = gab_ref[...]
    xg = gab + dt_ref[...]
    softplus = jnp.maximum(xg, 0.0) + jnp.log1p(jnp.exp(-jnp.abs(xg)))
    g_all = -jnp.exp(al_ref[...]) * softplus
    b_all = jax.nn.sigmoid(gab)

    g_t = g_all.T
    pos = lax.broadcasted_iota(jnp.int32, (1, seq), 1) % c
    yf = g_t
    yb = g_t
    s = 1
    while s < c:
        yf = yf + jnp.where(pos >= s, pltpu.roll(yf, s, 1), 0.0)
        yb = yb + jnp.where(pos < c - s, pltpu.roll(yb, seq - s, 1), 0.0)
        s *= 2
    rowi = lax.broadcasted_iota(jnp.int32, (LANES, 1), 0)
    cum_t = jnp.where(rowi < G_HEADS, yf, yb)
    cumt_ref[...] = cum_t
    cum_c = cum_t.T
    lane = lax.broadcasted_iota(jnp.int32, (1, LANES), 1)

    for d in range(2):
        r = d * G_HEADS + head
        gcol = jnp.sum(jnp.where(lane == r, cum_c, 0.0), axis=1, keepdims=True)
        bcol = jnp.sum(jnp.where(lane == 2 * G_HEADS + r, b_all, 0.0), axis=1, keepdims=True)
        grow = cumt_ref[pl.ds(r, 1), :]
        for n in range(n_chunks):
            sl = slice(n * c, (n + 1) * c)
            gcb_s[d, n] = jnp.broadcast_to(gcol[sl], (c, LANES))
            btb_s[d, n] = jnp.broadcast_to(bcol[sl], (c, LANES))
            grow_s[d, n] = jnp.broadcast_to(grow[:, sl], (SUBLANES, c))
    for n in range(n_chunks):
        sl = slice(n * c, (n + 1) * c)
        kc_s[n] = kn[sl]
        kt_s[n] = kn[sl].T
        qc_s[n] = qn[sl]
        vc_s[n] = v[sl]

    ri = lax.broadcasted_iota(jnp.int32, (c, c), 0)
    ci = lax.broadcasted_iota(jnp.int32, (c, c), 1)

    for d in range(2):
        incl = (ri >= ci) if d == 0 else (ri <= ci)
        strict = (ri > ci) if d == 0 else (ri < ci)

        def prep(n, carry, d=d, incl=incl, strict=strict):
            kc = kc_s[n]
            kt = kt_s[n]
            gcb = gcb_s[d, n]
            btb = btb_s[d, n]
            decay = jnp.exp(jnp.where(incl, gcb - grow_s[d, n][0:1, :], -jnp.inf))
            kbeta = kc * btb
            l_mat = jnp.where(strict, _mm(kbeta, kt) * decay, 0.0)
            inv = _unit_tri_inverse(l_mat)
            w_s[d, n] = _mm_f32(inv, kbeta * jnp.exp(gcb))
            u_s[d, n] = _mm_f32(inv, vc_s[n] * btb)
            qk_s[d, n] = jnp.where(incl, _mm(qc_s[n], kt) * decay, 0.0)
            return carry

        lax.fori_loop(0, n_chunks, prep, 0)

    for d in range(2):
        def scan(i, state, d=d):
            n = i if d == 0 else n_chunks - 1 - i
            gcb = gcb_s[d, n]
            glast = gcb[c - 1:c, :] if d == 0 else gcb[0:1, :]
            v_new = u_s[d, n] - _mm(w_s[d, n], state)
            out = _mm(qc_s[n] * jnp.exp(gcb), state) + _mm(qk_s[d, n], v_new)
            if d == 0:
                oacc_s[n] = out
            else:
                oacc_s[n] = oacc_s[n] + out
            kdec_t = kt_s[n] * jnp.exp(glast - grow_s[d, n][0:1, :])
            return state * jnp.exp(glast) + _mm(kdec_t, v_new)

        s_init = s0_ref[0, d, 0] if has_s0 else jnp.zeros((G_DK, LANES), F32)
        s_fin = lax.fori_loop(0, n_chunks, scan, s_init)
        if emit_state:
            sn_ref[0, d, 0] = s_fin

    ng = ng_ref[...]
    for n in range(n_chunks):
        sl = slice(n * c, (n + 1) * c)
        o = oacc_s[n]
        o = o * lax.rsqrt(jnp.mean(o * o, axis=-1, keepdims=True) + LN_EPS) * ng
        o_ref[sl, :] = (o * _silu(z_ref[sl, :])).astype(o_ref.dtype)


def _gdn_call(proj, cw, al, dt, ng, s0, row0, n_seq, seq, emit_state):
    n_chunks = seq // G_CHUNK
    col = lambda off: pl.BlockSpec((seq, LANES), lambda b, h, off=off: (row0 + b, off // LANES + h))
    cwspec = lambda part: pl.BlockSpec((G_SHORT, LANES), lambda b, h, part=part: (0, part * G_HEADS + h))
    vec = pl.BlockSpec((1, LANES), lambda b, h: (0, 0))
    in_specs = [col(OFF_GQ), col(OFF_GK), col(OFF_GV),
                pl.BlockSpec((seq, LANES), lambda b, h: (row0 + b, OFF_GAB // LANES)),
                col(OFF_GZ), cwspec(0), cwspec(1), cwspec(2), vec, vec, vec]
    args = [proj, proj, proj, proj, proj, cw, cw, cw, al, dt, ng]
    state_spec = pl.BlockSpec((1, 2, 1, G_DK, LANES), lambda b, h: (b, 0, h, 0, 0))
    if s0 is not None:
        in_specs.append(state_spec)
        args.append(s0)
    out_specs = [pl.BlockSpec((seq, LANES), lambda b, h: (b, h))]
    out_shape = [jax.ShapeDtypeStruct((n_seq * seq, G_HEADS * LANES), BF16)]
    if emit_state:
        out_specs.append(state_spec)
        out_shape.append(jax.ShapeDtypeStruct((n_seq, 2, G_HEADS, G_DK, LANES), F32))
    chunked = lambda lead: pltpu.VMEM(lead + (G_CHUNK, LANES), F32)
    scratch = [pltpu.VMEM((seq + 2 * SUBLANES, LANES), F32),
               chunked((n_chunks,)), chunked((n_chunks,)), chunked((n_chunks,)), chunked((n_chunks,)),
               chunked((2, n_chunks)), chunked((2, n_chunks)),
               pltpu.VMEM((2, n_chunks, SUBLANES, G_CHUNK), F32),
               chunked((2, n_chunks)), chunked((2, n_chunks)), chunked((2, n_chunks)),
               pltpu.VMEM((LANES, seq), F32), chunked((n_chunks,))]
    res = pl.pallas_call(
        functools.partial(_gdn_kernel, seq=seq, has_s0=s0 is not None, emit_state=emit_state),
        grid=(n_seq, G_HEADS),
        in_specs=in_specs,
        out_specs=out_specs,
        out_shape=out_shape,
        scratch_shapes=scratch,
        compiler_params=_cparams("arbitrary", "arbitrary"),
    )(*args)
    return res if emit_state else (res[0], None)


def _merge_kernel(x_ref, mg0_ref, mg1_ref, mg2_ref, oa_ref, oc_ref, og_ref, wa_ref, wc_ref, wg_ref,
                  wo_ref, mod_ref, lng_ref, lnb_ref, x1_ref, h2_ref, *, n_ctx_blocks, blocks_per_lat, alpha):
    row = _mod_row(pl.program_id(0), n_ctx_blocks, blocks_per_lat)
    g1 = mod_ref[pl.ds(row, 1), 2 * D_MODEL:3 * D_MODEL]
    sh2 = mod_ref[pl.ds(row, 1), 3 * D_MODEL:4 * D_MODEL]
    sc2 = mod_ref[pl.ds(row, 1), 4 * D_MODEL:5 * D_MODEL]
    merged = (jax.nn.sigmoid(mg0_ref[...]) * _mm(oa_ref[...], wa_ref[...])
              + jax.nn.sigmoid(mg1_ref[...]) * _mm(oc_ref[...], wc_ref[...])
              + jax.nn.sigmoid(mg2_ref[...]) * _mm(og_ref[...], wg_ref[...]))
    mix = _mm(merged, wo_ref[...])
    x1 = _layer_norm(alpha * x_ref[...] + g1 * mix, lng_ref[...], lnb_ref[...])
    x1_ref[...] = x1
    h2_ref[...] = (x1 * (1.0 + sc2) + sh2).astype(h2_ref.dtype)


def _merge_call(x, proj, oa, oc, og, wa, wc, wg, wo, mod_l, lng, lnb, n_ctx_tok, lat_seq, alpha):
    n_tok = x.shape[0]
    tb = TB_MERGE
    tok = lambda w: pl.BlockSpec((tb, w), lambda i: (i, 0))
    mg = lambda j: pl.BlockSpec((tb, D_MODEL), lambda i, j=j: (i, j))
    full = lambda a: pl.BlockSpec(a.shape, lambda i: (0, 0))
    kern = functools.partial(_merge_kernel, n_ctx_blocks=n_ctx_tok // tb, blocks_per_lat=lat_seq // tb,
                             alpha=alpha)
    return pl.pallas_call(
        kern,
        grid=(n_tok // tb,),
        in_specs=[tok(D_MODEL), mg(0), mg(1), mg(2), tok(512), tok(512), tok(512),
                  full(wa), full(wc), full(wg), full(wo), full(mod_l), full(lng), full(lnb)],
        out_specs=[tok(D_MODEL), tok(D_MODEL)],
        out_shape=[jax.ShapeDtypeStruct((n_tok, D_MODEL), F32), jax.ShapeDtypeStruct((n_tok, D_MODEL), BF16)],
        compiler_params=_cparams("arbitrary"),
    )(x, proj, proj, proj, oa, oc, og, wa, wc, wg, wo, mod_l, lng, lnb)


def _top16(s):
    rowi = lax.broadcasted_iota(jnp.int32, (P_TOPK, 1), 0)
    out = jnp.zeros((P_TOPK, s.shape[1]), F32)
    w = s
    for it in range(P_TOPK):
        m = jnp.max(w, axis=0, keepdims=True)
        out = jnp.where(rowi == it, m, out)
        if it + 1 < P_TOPK:
            w = jnp.where(w >= m, -jnp.inf, w)
    return out


def _peer_select(s1, s2):
    sv1 = _top16(s1)
    sv2 = _top16(s2)
    m1 = sv1[0:1]
    m2 = sv2[0:1]
    ea = jnp.exp(sv1 - m1)
    eb = jnp.exp(sv2 - m2)
    cands = [sv1 + sv2[0:1]]
    wts = [ea * eb[0:1]]
    for b in range(1, SUBLANES):
        cands.append(sv1[0:SUBLANES] + sv2[b:b + 1])
        wts.append(ea[0:SUBLANES] * eb[b:b + 1])
    cands.append(sv2[SUBLANES:] + sv1[0:1])
    wts.append(eb[SUBLANES:] * ea[0:1])
    work = list(cands)
    tau = None
    for it in range(P_TOPK):
        m = functools.reduce(jnp.maximum, [jnp.max(w, axis=0, keepdims=True) for w in work])
        if it + 1 < P_TOPK:
            work = [jnp.where(w >= m, -jnp.inf, w) for w in work]
        tau = m
    z = functools.reduce(lambda a, b: a + b,
                         [jnp.sum(jnp.where(cd >= tau, wt, 0.0), axis=0, keepdims=True)
                          for cd, wt in zip(cands, wts)])
    e1 = jnp.exp(s1 - m1) * (1.0 / z)
    e2 = jnp.exp(s2 - m2)
    return tau, e1, e2


def _peer_kernel(h2_ref, x1_ref, wqt_ref, keys_ref, u_ref, v_ref, mod_ref, lng_ref, lnb_ref, o_ref,
                 qt_ref, s_ref, e_ref, tau_ref, ga_ref, acc_ref, *, n_ctx_blocks, blocks_per_lat, alpha):
    j = pl.program_id(1)
    last = pl.num_programs(1) - 1
    row = _mod_row(pl.program_id(0), n_ctx_blocks, blocks_per_lat)
    tb = TB_PEER
    n_col = tb // LANES

    @pl.when(j == 0)
    def _():
        qt_ref[...] = _mm_nt(wqt_ref[...], h2_ref[...])
        for hh in range(P_HEADS):
            for p in range(2):
                r0 = (hh * 2 + p) * N_KEYS
                s_ref[hh, p] = _mm(keys_ref[p], qt_ref[r0:r0 + N_KEYS, :])
        for cc in range(n_col):
            cs = slice(cc * LANES, (cc + 1) * LANES)
            for hh in range(P_HEADS):
                tau, e1, e2 = _peer_select(s_ref[hh, 0, :, cs], s_ref[hh, 1, :, cs])
                tau_ref[hh:hh + 1, cs] = tau
                e_ref[hh, 0, :, cs] = e1
                e_ref[hh, 1, :, cs] = e2
        acc_ref[...] = jnp.zeros_like(acc_ref)

    act = _mm_nt(u_ref[...], h2_ref[...])
    gel = 0.5 * act * (1.0 + lax.erf(act * (2.0 ** -0.5)))
    i1_base = pl.multiple_of(j * SUBLANES, SUBLANES)
    for cc in range(n_col):
        cs = slice(cc * LANES, (cc + 1) * LANES)
        for half in range(EB_PEER // N_KEYS):
            g = jnp.zeros((N_KEYS, LANES), F32)
            for hh in range(P_HEADS):
                s1row = s_ref[hh, 0, pl.ds(i1_base, SUBLANES), cs][half:half + 1]
                e1row = e_ref[hh, 0, pl.ds(i1_base, SUBLANES), cs][half:half + 1]
                sel = (s1row + s_ref[hh, 1, :, cs]) >= tau_ref[hh:hh + 1, cs]
                g = g + jnp.where(sel, e1row * e_ref[hh, 1, :, cs], 0.0)
            ga_ref[half * N_KEYS:(half + 1) * N_KEYS, cs] = (
                g * gel[half * N_KEYS:(half + 1) * N_KEYS, cs]).astype(BF16)
    acc_ref[...] += lax.dot_general(ga_ref[...], v_ref[...], (((0,), (0,)), ((), ())),
                                    preferred_element_type=F32)

    @pl.when(j == last)
    def _():
        g2 =mod_ref[pl.ds(row, 1), 5 * D_MODEL:6 * D_MODEL]
        o_ref[...] = _layer_norm(alpha * x1_ref[...] + g2 * acc_ref[...], lng_ref[...], lnb_ref[...])


def _peer_call(h2, x1, wqt, keys, u_tab, v_tab, mod_l, lng, lnb, n_ctx_tok, lat_seq, alpha):
    n_tok = h2.shape[0]
    tb = TB_PEER
    n_exp = u_tab.shape[0]
    tok = pl.BlockSpec((tb, D_MODEL), lambda i, j: (i, 0))
    full = lambda a: pl.BlockSpec(a.shape, lambda i, j: (0,) * a.ndim)
    tab = pl.BlockSpec((EB_PEER, D_MODEL), lambda i, j: (j, 0))
    kern = functools.partial(_peer_kernel, n_ctx_blocks=n_ctx_tok // tb, blocks_per_lat=lat_seq // tb,
                             alpha=alpha)
    return pl.pallas_call(
        kern,
        grid=(n_tok // tb, n_exp // EB_PEER),
        in_specs=[tok, tok, full(wqt), full(keys), tab, tab, full(mod_l), full(lng), full(lnb)],
        out_specs=tok,
        out_shape=jax.ShapeDtypeStruct((n_tok, D_MODEL), F32),
        scratch_shapes=[pltpu.VMEM((P_HEADS * 2 * N_KEYS, tb), F32),
                        pltpu.VMEM((P_HEADS, 2, N_KEYS, tb), F32),
                        pltpu.VMEM((P_HEADS, 2, N_KEYS, tb), F32),
                        pltpu.VMEM((P_HEADS, tb), F32),
                        pltpu.VMEM((EB_PEER, tb), BF16),
                        pltpu.VMEM((tb, D_MODEL), F32)],
        compiler_params=_cparams("arbitrary", "arbitrary"),
    )(h2, x1, wqt, keys, u_tab, v_tab, mod_l, lng, lnb)


def _rope_tables(n_tokens):
    n_rows = n_tokens // GRID_W
    rows = np.repeat(np.arange(n_rows, dtype=np.float32), GRID_W)
    cols = np.tile(np.arange(GRID_W, dtype=np.float32), n_rows)
    half = A_DQK // 2
    inv_freq = (1.0 / (ROPE_BASE ** (jnp.arange(0, half, 2, dtype=F32) / half)))
    ang_r = jnp.asarray(rows)[:, None] * inv_freq
    ang_c = jnp.asarray(cols)[:, None] * inv_freq
    zeros = jnp.zeros_like(ang_r)
    cr, sr, cc, sc = jnp.cos(ang_r), jnp.sin(ang_r), jnp.cos(ang_c), jnp.sin(ang_c)
    cos = jnp.concatenate([cr, cr, cc, cc], axis=-1)
    s_lo = jnp.concatenate([-sr, zeros, -sc, zeros], axis=-1)
    s_hi = jnp.concatenate([zeros, sr, zeros, sc], axis=-1)
    reps = 2 * A_HEADS
    return tuple(jnp.tile(t, (1, reps)) for t in (cos, s_lo, s_hi))


def _lane_vec(a):
    flat = a.reshape(1, -1).astype(F32)
    return jnp.pad(flat, ((0, 0), (0, LANES - flat.shape[1])))


def kernel(x_prompt, x_sample, cache_attn_k, cache_attn_v, state_gdn, c, c_ctx, w_mod, b_mod, w_in,
           diff_lambda, diff_norm_g, w_attn_o, conv_dw_w, conv_dw_b, conv_ln_g, conv_ln_b, w_conv_o,
           gdn_conv_w, gdn_A_log, gdn_dt_bias, gdn_norm_g, w_gdn_o, w_out, ln_g, ln_b,
           peer_wq, peer_keys, peer_u, peer_v):
    b_ctx, seq, d = x_prompt.shape
    b_lat, lat_seq, _ = x_sample.shape
    depth = w_mod.shape[0]
    past = cache_attn_k.shape[2]
    n_ctx_tok = b_ctx * seq
    alpha = (2 * depth) ** 0.25

    sizes = np.cumsum([0, 512, 512, 512, 1024, 1536, 512, 16, 3072])
    aq0, ak0, av0, cin0, gqkv0, gz0, gab0, mg0, end = [int(s) for s in sizes]
    w_p = jnp.concatenate([
        w_in[:, :, mg0:end], w_in[:, :, cin0:gqkv0], w_in[:, :, aq0:cin0], w_in[:, :, gz0:gab0],
        w_in[:, :, gqkv0:gz0], w_in[:, :, gab0:mg0],
        jnp.zeros((depth, d, N_PROJ - int(end)), w_in.dtype)], axis=-1).astype(BF16)
    wa_b, wc_b, wg_b, wo_b = (w.astype(BF16) for w in (w_attn_o, w_conv_o, w_gdn_o, w_out))
    wqt_b = jnp.swapaxes(peer_wq, 1, 2).astype(BF16)
    keys_b = peer_keys.astype(BF16)
    u_b = peer_u.astype(BF16)
    v_b = peer_v.astype(BF16)
    conv_w_p = jnp.pad(conv_dw_w, ((0, 0), (0, 32 - CONV_W), (0, 0)))
    rope_tabs = _rope_tables(lat_seq)
    cache_k = cache_attn_k.reshape(b_lat, depth, past, A_HEADS * LANES)
    cache_v = cache_attn_v.reshape(b_lat, depth, past, A_HEADS * LANES)

    cond8 = jnp.concatenate([c_ctx[None, :], c, jnp.zeros((SUBLANES - 1 - b_lat, d), F32)], axis=0)
    mod_all = _mod_call(cond8, w_mod, b_mod)

    x = jnp.concatenate([x_prompt.reshape(n_ctx_tok, d), x_sample.reshape(b_lat * lat_seq, d)], axis=0)
    k_list, v_list, s_list = [], [], []
    for l in range(depth):
        mod_l = mod_all[l]
        lam_init = 0.8 - 0.6 * math.exp(-0.3 * l)
        proj = _inproj_call(x, mod_l, w_p[l], n_ctx_tok, lat_seq)
        k_list.append(proj[:n_ctx_tok, OFF_AK:OFF_AK + 512].reshape(b_ctx, seq, A_HEADS, 2 * A_DQK))
        v_list.append(proj[:n_ctx_tok, OFF_AV:OFF_AV + 512].reshape(b_ctx, seq, A_HEADS, A_DV))

        dl = diff_lambda[l]
        ng_a = diff_norm_g[l].reshape(1, LANES)
        oa_ctx = _attn_ctx_call(proj, dl, ng_a, b_ctx, seq, lam_init)
        oa_lat = _attn_lat_call(proj, cache_k, cache_v, l, rope_tabs, dl, ng_a, n_ctx_tok, b_lat, lat_seq,
                                lam_init)

        cvec = lambda a: a[l].reshape(1, C_CONV)
        cargs = (conv_w_p[l], cvec(conv_dw_b), cvec(conv_ln_g), cvec(conv_ln_b))
        oc_ctx = _conv_call(proj, *cargs, 0, b_ctx, seq)
        oc_lat = _conv_call(proj, *cargs, n_ctx_tok // lat_seq, b_lat, lat_seq)

        gargs = (gdn_conv_w[l], _lane_vec(gdn_A_log[l]), _lane_vec(gdn_dt_bias[l]),
                 gdn_norm_g[l].reshape(1, LANES))
        og_ctx, s_new = _gdn_call(proj, *gargs, None, 0, b_ctx, seq, True)
        og_lat, _ = _gdn_call(proj, *gargs, state_gdn[:, l], n_ctx_tok // lat_seq, b_lat, lat_seq, False)
        s_list.append(s_new)

        oa = jnp.concatenate([oa_ctx, oa_lat], axis=0)
        oc = jnp.concatenate([oc_ctx, oc_lat], axis=0)
        og = jnp.concatenate([og_ctx, og_lat], axis=0)
        x1, h2 = _merge_call(x, proj, oa, oc, og, wa_b[l], wc_b[l], wg_b[l], wo_b[l], mod_l,
                             ln_g[l, 0:1], ln_b[l, 0:1], n_ctx_tok, lat_seq, alpha)
        x = _peer_call(h2, x1, wqt_b[l], keys_b[l], u_b[l], v_b[l], mod_l, ln_g[l, 1:2], ln_b[l, 1:2],
                       n_ctx_tok, lat_seq, alpha)

    y_prompt = x[:n_ctx_tok].reshape(b_ctx, seq, d)
    y_sample = x[n_ctx_tok:].reshape(b_lat, lat_seq, d)
    return (y_prompt, y_sample, jnp.stack(k_list, axis=1), jnp.stack(v_list, axis=1),
            jnp.stack(s_list, axis=1))
```

```python
import functools
import math

import numpy as np
import jax
import jax.numpy as jnp
from jax import lax
from jax.experimental import pallas as pl
from jax.experimental.pallas import tpu as pltpu

F32 = jnp.float32
BF16 = jnp.bfloat16

LANES = 128
SUBLANES = 8
VMEM_LIMIT = 56 * 1024 * 1024

D_MODEL = 1024
A_HEADS = 4
A_DQK = 64
A_DV = 128
GRID_W = 64
ROPE_BASE = 10000.0
C_CONV = 512
CONV_W = 31
G_HEADS = 4
G_DK = 128
G_SHORT = 3
G_CHUNK = 128
P_HEADS = 8
N_KEYS = 128
P_TOPK = 16
LN_EPS = 1e-5

N_PROJ = 8192
OFF_MG, OFF_CA, OFF_CB, OFF_AQ, OFF_AK, OFF_AV, OFF_GZ, OFF_GQ, OFF_GK, OFF_GV, OFF_GAB = (
    0, 3072, 3584, 4096, 4608, 5120, 5632, 6144, 6656, 7168, 7680)

TB_PROJ = 512
TN_PROJ = 2048
TB_MERGE = 256
TB_PEER = 512
EB_PEER = 1024


def _cparams(*sem):
    return pltpu.CompilerParams(dimension_semantics=sem, vmem_limit_bytes=VMEM_LIMIT)


def _mm(a, b):
    return jnp.dot(a.astype(BF16), b.astype(BF16), preferred_element_type=F32)


def _mm_nt(a, b):
    return lax.dot_general(a.astype(BF16), b.astype(BF16), (((1,), (1,)), ((), ())),
                           preferred_element_type=F32)


def _silu(x):
    return x * jax.nn.sigmoid(x)


def _mod_row(i, n_ctx_blocks, blocks_per_lat):
    return jnp.where(i < n_ctx_blocks, 0, 1 + (i - n_ctx_blocks) // blocks_per_lat)


def _layer_norm(x, g, b):
    mu = jnp.mean(x, axis=-1, keepdims=True)
    xc = x - mu
    var = jnp.mean(xc * xc, axis=-1, keepdims=True)
    return xc * lax.rsqrt(var + LN_EPS) * g + b


def _mod_kernel(cond_ref, w_ref, b_ref, o_ref):
    a = _silu(cond_ref[...])
    o_ref[0] = _mm(a, w_ref[0]) + b_ref[0]


def _mod_call(cond8, w_mod, b_mod):
    depth, d, n = w_mod.shape
    tn = 1536
    return pl.pallas_call(
        _mod_kernel,
        grid=(depth, n // tn),
        in_specs=[pl.BlockSpec((SUBLANES, d), lambda l, j: (0, 0)),
                  pl.BlockSpec((1, d, tn), lambda l, j: (l, 0, j)),
                  pl.BlockSpec((1, 1, tn), lambda l, j: (l, 0, j))],
        out_specs=pl.BlockSpec((1, SUBLANES, tn), lambda l, j: (l, 0, j)),
        out_shape=jax.ShapeDtypeStruct((depth, SUBLANES, n), F32),
        compiler_params=_cparams("arbitrary", "arbitrary"),
    )(cond8, w_mod, b_mod.reshape(depth, 1, n))


def _inproj_kernel(x_ref, mod_ref, w_ref, o_ref, *, n_ctx_blocks, blocks_per_lat):
    row = _mod_row(pl.program_id(1), n_ctx_blocks, blocks_per_lat)
    sh = mod_ref[pl.ds(row, 1), 0:D_MODEL]
    sc = mod_ref[pl.ds(row, 1), D_MODEL:2 * D_MODEL]
    h = x_ref[...] * (1.0 + sc) + sh
    o_ref[...] = _mm(h, w_ref[...])


def _inproj_call(x, mod_l, w_p, n_ctx_tok, lat_seq):
    n_tok = x.shape[0]
    kern = functools.partial(_inproj_kernel, n_ctx_blocks=n_ctx_tok // TB_PROJ,
                             blocks_per_lat=lat_seq // TB_PROJ)
    return pl.pallas_call(
        kern,
        grid=(N_PROJ // TN_PROJ, n_tok // TB_PROJ),
        in_specs=[pl.BlockSpec((TB_PROJ, D_MODEL), lambda j, i: (i, 0)),
                  pl.BlockSpec((SUBLANES, 6 * D_MODEL), lambda j, i: (0, 0)),
                  pl.BlockSpec((D_MODEL, TN_PROJ), lambda j, i: (0, j))],
        out_specs=pl.BlockSpec((TB_PROJ, TN_PROJ), lambda j, i: (i, j)),
        out_shape=jax.ShapeDtypeStruct((n_tok, N_PROJ), F32),
        compiler_params=_cparams("arbitrary", "arbitrary"),
    )(x, mod_l, w_p)


def _softmax(s):
    e = jnp.exp(s - jnp.max(s, axis=-1, keepdims=True))
    return e / jnp.sum(e, axis=-1, keepdims=True)


def _diff_lambda(dl_ref, lam_init):
    dl = dl_ref[...]
    a = jnp.sum(dl[0:1] * dl[1:2], axis=-1, keepdims=True)
    b = jnp.sum(dl[2:3] * dl[3:4], axis=-1, keepdims=True)
    return jnp.exp(a) - jnp.exp(b) + lam_init


def _diff_attn_heads(q, k_of, v_of, lam, ng, lam_init, o_ref):
    lo = lax.broadcasted_iota(jnp.int32, (1, LANES), 1) < A_DQK
    scale = A_DQK ** -0.5
    for h in range(A_HEADS):
        qh = q[:, h * LANES:(h + 1) * LANES]
        kh = k_of(h)
        s1 = _mm_nt(jnp.where(lo, qh, 0.0), kh) * scale
        s2 = _mm_nt(jnp.where(lo, 0.0, qh), kh) * scale
        a = _softmax(s1) - lam * _softmax(s2)
        o = _mm(a, v_of(h))
        o = o * lax.rsqrt(jnp.mean(o * o, axis=-1, keepdims=True) + LN_EPS) * ng
        o_ref[:, h * LANES:(h + 1) * LANES] = (o * (1.0 - lam_init)).astype(o_ref.dtype)


def _attn_ctx_kernel(q_ref, k_ref, v_ref, dl_ref, ng_ref, o_ref, *, lam_init):
    lam = _diff_lambda(dl_ref, lam_init)
    _diff_attn_heads(q_ref[...],
                     lambda h: k_ref[:, h * LANES:(h + 1) * LANES].astype(BF16),
                     lambda h: v_ref[:, h * LANES:(h + 1) * LANES].astype(BF16),
                     lam, ng_ref[...], lam_init, o_ref)


def _attn_ctx_call(proj, dl, ng, n_seq, seq, lam_init):
    w = A_HEADS * LANES
    spec = lambda off: pl.BlockSpec((seq, w), lambda b, off=off: (b, off // w))
    return pl.pallas_call(
        functools.partial(_attn_ctx_kernel, lam_init=lam_init),
        grid=(n_seq,),
        in_specs=[spec(OFF_AQ), spec(OFF_AK), spec(OFF_AV),
                  pl.BlockSpec(dl.shape, lambda b: (0, 0)),
                  pl.BlockSpec((1, LANES), lambda b: (0, 0))],
        out_specs=pl.BlockSpec((seq, w), lambda b: (b, 0)),
        out_shape=jax.ShapeDtypeStruct((n_seq * seq, w), BF16),
        compiler_params=_cparams("arbitrary"),
    )(proj, proj, proj, dl, ng)


def _rope(x, cos, sin_lo, sin_hi):
    n = x.shape[-1]
    return x * cos + pltpu.roll(x, n - 16, 1) * sin_lo + pltpu.roll(x, 16, 1) * sin_hi


def _attn_lat_kernel(q_ref, k_ref, v_ref, ck_ref, cv_ref, cosq_ref, slq_ref, shq_ref,
                     cosk_ref, slk_ref, shk_ref, dl_ref, ng_ref, o_ref, kall_ref, vall_ref,
                     *, lam_init, past):
    @pl.when(pl.program_id(1) == 0)
    def _():
        kall_ref[0:past, :] = ck_ref[0, 0].astype(BF16)
        vall_ref[0:past, :] = cv_ref[0, 0].astype(BF16)
        kall_ref[past:, :] = _rope(k_ref[...], cosk_ref[...], slk_ref[...], shk_ref[...]).astype(BF16)
        vall_ref[past:, :] = v_ref[...].astype(BF16)

    lam = _diff_lambda(dl_ref, lam_init)
    q = _rope(q_ref[...], cosq_ref[...], slq_ref[...], shq_ref[...])
    _diff_attn_heads(q,
                     lambda h: kall_ref[:, h * LANES:(h + 1) * LANES],
                     lambda h: vall_ref[:, h * LANES:(h + 1) * LANES],
                     lam, ng_ref[...], lam_init, o_ref)


def _attn_lat_call(proj, cache_k, cache_v, layer, rope_tabs, dl, ng, n_ctx_tok, n_seq, seq, lam_init):
    w = A_HEADS * LANES
    past = cache_k.shape[2]
    qb = 256
    nqb = seq // qb
    row0 = n_ctx_tok // seq
    rowq0 = n_ctx_tok // qb
    cos, s_lo, s_hi = rope_tabs
    qspec = pl.BlockSpec((qb, w), lambda b, i: (rowq0 + b * nqb + i, OFF_AQ // w))
    kspec = pl.BlockSpec((seq, w), lambda b, i: (row0 + b, OFF_AK // w))
    vspec = pl.BlockSpec((seq, w), lambda b, i: (row0 + b, OFF_AV // w))
    cspec = pl.BlockSpec((1, 1, past, w), lambda b, i: (b, layer, 0, 0))
    tq = pl.BlockSpec((qb, w), lambda b, i: (i, 0))
    tk = pl.BlockSpec((seq, w), lambda b, i: (0, 0))
    return pl.pallas_call(
        functools.partial(_attn_lat_kernel, lam_init=lam_init, past=past),
        grid=(n_seq, nqb),
        in_specs=[qspec, kspec, vspec, cspec, cspec, tq, tq, tq, tk, tk, tk,
                  pl.BlockSpec(dl.shape, lambda b, i: (0, 0)),
                  pl.BlockSpec((1, LANES), lambda b, i: (0, 0))],
        out_specs=pl.BlockSpec((qb, w), lambda b, i: (b * nqb + i, 0)),
        out_shape=jax.ShapeDtypeStruct((n_seq * seq, w), BF16),
        scratch_shapes=[pltpu.VMEM((past + seq, w), BF16), pltpu.VMEM((past + seq, w), BF16)],
        compiler_params=_cparams("arbitrary", "arbitrary"),
    )(proj, proj, proj, cache_k, cache_v, cos, s_lo, s_hi, cos, s_lo, s_hi, dl, ng)


CONV_PAD = 16
CONV_ROWS = 64


def _conv_kernel(ca_ref, cb_ref, w_ref, b_ref, lg_ref, lb_ref, o_ref, pad_ref, *, seq):
    zeros = jnp.zeros((CONV_PAD, C_CONV), F32)
    pad_ref[0:CONV_PAD, :] = zeros
    pad_ref[seq + CONV_PAD:seq + 2 * CONV_PAD, :] = zeros
    pad_ref[CONV_PAD:seq + CONV_PAD, :] = ca_ref[...] * jax.nn.sigmoid(cb_ref[...])
    half = CONV_W // 2

    def body(ci, carry):
        base = pl.multiple_of(ci * CONV_ROWS, CONV_ROWS)
        win = pad_ref[pl.ds(base, CONV_ROWS + 2 * CONV_PAD), :]
        acc = jnp.zeros((CONV_ROWS, C_CONV), F32)
        for r in range(SUBLANES):
            shifted = win[r:r + CONV_ROWS + 3 * SUBLANES]
            for qq in range(4):
                j = qq * SUBLANES + r - (CONV_PAD - half)
                if 0 <= j < CONV_W:
                    acc = acc + w_ref[j:j + 1, :] * shifted[qq * SUBLANES:qq * SUBLANES + CONV_ROWS]
        y = _layer_norm(acc + b_ref[...], lg_ref[...], lb_ref[...])
        o_ref[pl.ds(base, CONV_ROWS), :] = _silu(y).astype(o_ref.dtype)
        return carry

    lax.fori_loop(0, seq // CONV_ROWS, body, 0)


def _conv_call(proj, w, b, lg, lb, row0, n_seq, seq):
    spec = lambda off: pl.BlockSpec((seq, C_CONV), lambda s, off=off: (row0 + s, off // C_CONV))
    vec = pl.BlockSpec((1, C_CONV), lambda s: (0, 0))
    return pl.pallas_call(
        functools.partial(_conv_kernel, seq=seq),
        grid=(n_seq,),
        in_specs=[spec(OFF_CA), spec(OFF_CB), pl.BlockSpec(w.shape, lambda s: (0, 0)), vec, vec, vec],
        out_specs=pl.BlockSpec((seq, C_CONV), lambda s: (s, 0)),
        out_shape=jax.ShapeDtypeStruct((n_seq * seq, C_CONV), BF16),
        scratch_shapes=[pltpu.VMEM((seq + 2 * CONV_PAD, C_CONV), F32)],
        compiler_params=_cparams("arbitrary"),
    )(proj, proj, w, b, lg, lb)


GDN_GROUP = 4


def _split(a):
    hi = a.astype(BF16)
    return hi, (a - hi.astype(F32)).astype(BF16)


def _mm_split(a, b):
    (ah, al), (bh, bl) = a, b
    dot = lambda x, y: jnp.dot(x, y, preferred_element_type=F32)
    return dot(ah, bh) + (dot(ah, bl) + dot(al, bh))


def _unit_tri_inverses(l_mats):
    n = l_mats[0].shape[0]
    eye = (lax.broadcasted_iota(jnp.int32, (n, n), 0) ==
           lax.broadcasted_iota(jnp.int32, (n, n), 1)).astype(F32)
    ps = [eye - l for l in l_mats]
    ms = [_split(l) for l in l_mats]
    for _ in range(int(math.log2(n)) - 1):
        ms = [_split(_mm_split(m, m)) for m in ms]
        ps = [p + _mm_split(_split(p), m) for p, m in zip(ps, ms)]
    return ps


def _gdn_kernel(*refs, seq, has_s0, emit_state):
    (q_ref, k_ref, v_ref, gab_ref, z_ref, cwq_ref, cwk_ref, cwv_ref, al_ref, dt_ref, ng_ref) = refs[:11]
    idx = 11
    s0_ref = None
    if has_s0:
        s0_ref = refs[idx]
        idx += 1
    o_ref = refs[idx]
    idx += 1
    sn_ref = None
    if emit_state:
        sn_ref = refs[idx]
        idx += 1
    (pad_ref, kc_s, kt_s, qc_s, vc_s, gcb_s, btb_s, grow_s, w_s, u_s, qk_s, cumt_ref, oacc_s) = refs[idx:]

    c = G_CHUNK
    n_chunks = seq // c
    head = pl.program_id(1)

    zeros8 = jnp.zeros((SUBLANES, LANES), F32)
    pad_ref[0:SUBLANES, :] = zeros8
    pad_ref[seq + SUBLANES:seq + 2 * SUBLANES, :] = zeros8

    def short_conv(x_ref, w_ref):
        pad_ref[SUBLANES:seq + SUBLANES, :] = x_ref[...]
        y = (w_ref[0:1, :] * pad_ref[SUBLANES - 1:seq + SUBLANES - 1, :]
             + w_ref[1:2, :] * pad_ref[SUBLANES:seq + SUBLANES, :]
             + w_ref[2:3, :] * pad_ref[SUBLANES + 1:seq + SUBLANES + 1, :])
        return _silu(y)

    q = short_conv(q_ref, cwq_ref)
    k = short_conv(k_ref, cwk_ref)
    v = short_conv(v_ref, cwv_ref)
    qn = q * lax.rsqrt(jnp.sum(q * q, axis=-1, keepdims=True) + 1e-6) * (G_DK ** -0.5)
    kn = k * lax.rsqrt(jnp.sum(k * k, axis=-1, keepdims=True) + 1e-6)

    gab = gab_ref[...]
    xg = gab + dt_ref[...]
    softplus = jnp.maximum(xg, 0.0) + jnp.log1p(jnp.exp(-jnp.abs(xg)))
    g_all = -jnp.exp(al_ref[...]) * softplus
    b_all = jax.nn.sigmoid(gab)

    g_t = g_all.T
    pos = lax.broadcasted_iota(jnp.int32, (1, seq), 1) % c
    yf = g_t
    yb = g_t
    s = 1
    while s < c:
        yf = yf + jnp.where(pos >= s, pltpu.roll(yf, s, 1), 0.0)
        yb = yb + jnp.where(pos < c - s, pltpu.roll(yb, seq - s, 1), 0.0)
        s *= 2
    rowi = lax.broadcasted_iota(jnp.int32, (LANES, 1), 0)
    cum_t = jnp.where(rowi < G_HEADS, yf, yb)
    cumt_ref[...] = cum_t
    cum_c = cum_t.T
    lane = lax.broadcasted_iota(jnp.int32, (1, LANES), 1)

    for d in range(2):
        r = d * G_HEADS + head
        gcol = jnp.sum(jnp.where(lane == r, cum_c, 0.0), axis=1, keepdims=True)
        bcol = jnp.sum(jnp.where(lane == 2 * G_HEADS + r, b_all, 0.0), axis=1, keepdims=True)
        grow = cumt_ref[pl.ds(r, 1), :]
        for n in range(n_chunks):
            sl = slice(n * c, (n + 1) * c)
            gcb_s[d, n] = jnp.broadcast_to(gcol[sl], (c, LANES))
            btb_s[d, n] = jnp.broadcast_to(bcol[sl], (c, LANES))
            grow_s[d, n] = jnp.broadcast_to(grow[:, sl], (SUBLANES, c))
    for n in range(n_chunks):
        sl = slice(n * c, (n + 1) * c)
        kc_s[n] = kn[sl]
        kt_s[n] = kn[sl].T
        qc_s[n] = qn[sl]
        vc_s[n] = v[sl]

    ri = lax.broadcasted_iota(jnp.int32, (c, c), 0)
    ci = lax.broadcasted_iota(jnp.int32, (c, c), 1)

    incl = [ri >= ci, ri <= ci]
    strict = [ri > ci, ri < ci]

    instances = [(d, n) for n in range(n_chunks) for d in range(2)]
    for g0 in range(0, len(instances), GDN_GROUP):
        group = instances[g0:g0 + GDN_GROUP]
        l_mats, rhs_w, rhs_u = [], [], []
        for d, n in group:
            kt = kt_s[n]
            gcb = gcb_s[d, n]
            btb = btb_s[d, n]
            decay = jnp.exp(jnp.where(incl[d], gcb - grow_s[d, n][0:1, :], -jnp.inf))
            kbeta = kc_s[n] * btb
            l_mats.append(jnp.where(strict[d], _mm(kbeta, kt) * decay, 0.0))
            rhs_w.append(_split(kbeta * jnp.exp(gcb)))
            rhs_u.append(_split(vc_s[n] * btb))
            qk_s[d, n] = jnp.where(incl[d], _mm(qc_s[n], kt) * decay, 0.0)
        invs = [_split(p) for p in _unit_tri_inverses(l_mats)]
        for (d, n), inv, rw, ru in zip(group, invs, rhs_w, rhs_u):
            w_s[d, n] = _mm_split(inv, rw)
            u_s[d, n] = _mm_split(inv, ru)

    states = [s0_ref[0, d, 0] if has_s0 else jnp.zeros((G_DK, LANES), F32) for d in range(2)]
    for i in range(n_chunks):
        for d in range(2):
            n = i if d == 0 else n_chunks - 1 - i
            gcb = gcb_s[d, n]
            glast = gcb[c - 1:c, :] if d == 0 else gcb[0:1, :]
            v_new = u_s[d, n] - _mm(w_s[d, n], states[d])
            oacc_s[d, n] = _mm(qc_s[n] * jnp.exp(gcb), states[d]) + _mm(qk_s[d, n], v_new)
            kdec_t = kt_s[n] * jnp.exp(glast - grow_s[d, n][0:1, :])
            states[d] = states[d] * jnp.exp(glast) + _mm(kdec_t, v_new)
    if emit_state:
        for d in range(2):
            sn_ref[0, d, 0] = states[d]

    ng = ng_ref[...]
    for n in range(n_chunks):
        sl = slice(n * c, (n + 1) * c)
        o = oacc_s[0, n] + oacc_s[1, n]
        o = o * lax.rsqrt(jnp.mean(o * o, axis=-1, keepdims=True) + LN_EPS) * ng
        o_ref[sl, :] = (o * _silu(z_ref[sl, :])).astype(o_ref.dtype)


def _gdn_call(proj, cw, al, dt, ng, s0, row0, n_seq, seq, emit_state):
    n_chunks = seq // G_CHUNK
    col = lambda off: pl.BlockSpec((seq, LANES), lambda b, h, off=off: (row0 + b, off // LANES + h))
    cwspec = lambda part: pl.BlockSpec((G_SHORT, LANES), lambda b, h, part=part: (0, part * G_HEADS + h))
    vec = pl.BlockSpec((1, LANES), lambda b, h: (0, 0))
    in_specs = [col(OFF_GQ), col(OFF_GK), col(OFF_GV),
                pl.BlockSpec((seq, LANES), lambda b, h: (row0 + b, OFF_GAB // LANES)),
                col(OFF_GZ), cwspec(0), cwspec(1), cwspec(2), vec, vec, vec]
    args = [proj, proj, proj, proj, proj, cw, cw, cw, al, dt, ng]
    state_spec = pl.BlockSpec((1, 2, 1, G_DK, LANES), lambda b, h: (b, 0, h, 0, 0))
    if s0 is not None:
        in_specs.append(state_spec)
        args.append(s0)
    out_specs = [pl.BlockSpec((seq, LANES), lambda b, h: (b, h))]
    out_shape = [jax.ShapeDtypeStruct((n_seq * seq, G_HEADS * LANES), BF16)]
    if emit_state:
        out_specs.append(state_spec)
        out_shape.append(jax.ShapeDtypeStruct((n_seq, 2, G_HEADS, G_DK, LANES), F32))
    chunked = lambda lead: pltpu.VMEM(lead + (G_CHUNK, LANES), F32)
    scratch = [pltpu.VMEM((seq + 2 * SUBLANES, LANES), F32),
               chunked((n_chunks,)), chunked((n_chunks,)), chunked((n_chunks,)), chunked((n_chunks,)),
               chunked((2, n_chunks)), chunked((2, n_chunks)),
               pltpu.VMEM((2, n_chunks, SUBLANES, G_CHUNK), F32),
               chunked((2, n_chunks)), chunked((2, n_chunks)), chunked((2, n_chunks)),
               pltpu.VMEM((LANES, seq), F32), chunked((2, n_chunks))]
    res = pl.pallas_call(
        functools.partial(_gdn_kernel, seq=seq, has_s0=s0 is not None, emit_state=emit_state),
        grid=(n_seq, G_HEADS),
        in_specs=in_specs,
        out_specs=out_specs,
        out_shape=out_shape,
        scratch_shapes=scratch,
        compiler_params=_cparams("arbitrary", "arbitrary"),
    )(*args)
    return res if emit_state else (res[0], None)


def _merge_kernel(x_ref, mg0_ref, mg1_ref, mg2_ref, oac_ref, oal_ref, occ_ref, ocl_ref, ogc_ref, ogl_ref,
                  wa_ref, wc_ref, wg_ref, wo_ref, mod_ref, lng_ref, lnb_ref, x1_ref, h2_ref,
                  *, n_ctx_blocks, blocks_per_lat, alpha):
    i = pl.program_id(0)
    row = _mod_row(i, n_ctx_blocks, blocks_per_lat)
    g1 = mod_ref[pl.ds(row, 1), 2 * D_MODEL:3 * D_MODEL]
    sh2 = mod_ref[pl.ds(row, 1), 3 * D_MODEL:4 * D_MODEL]
    sc2 = mod_ref[pl.ds(row, 1), 4 * D_MODEL:5 * D_MODEL]
    is_ctx = i < n_ctx_blocks
    pick = lambda c_ref, l_ref: jnp.where(is_ctx, c_ref[...], l_ref[...])
    merged = (jax.nn.sigmoid(mg0_ref[...]) * _mm(pick(oac_ref, oal_ref), wa_ref[...])
              + jax.nn.sigmoid(mg1_ref[...]) * _mm(pick(occ_ref, ocl_ref), wc_ref[...])
              + jax.nn.sigmoid(mg2_ref[...]) * _mm(pick(ogc_ref, ogl_ref), wg_ref[...]))
    mix = _mm(merged, wo_ref[...])
    x1 = _layer_norm(alpha * x_ref[...] + g1 * mix, lng_ref[...], lnb_ref[...])
    x1_ref[...] = x1
    h2_ref[...] = (x1 * (1.0 + sc2) + sh2).astype(h2_ref.dtype)


def _merge_call(x, proj, branches, wa, wc, wg, wo, mod_l, lng, lnb, n_ctx_tok, lat_seq, alpha):
    n_tok = x.shape[0]
    tb = TB_MERGE
    n_cb = n_ctx_tok // tb
    tok = lambda w: pl.BlockSpec((tb, w), lambda i: (i, 0))
    mg = lambda j: pl.BlockSpec((tb, D_MODEL), lambda i, j=j: (i, j))
    full = lambda a: pl.BlockSpec(a.shape, lambda i: (0, 0))
    ctx = pl.BlockSpec((tb, 512), lambda i: (jnp.minimum(i, n_cb - 1), 0))
    lat = pl.BlockSpec((tb, 512), lambda i: (jnp.maximum(i - n_cb, 0), 0))
    kern = functools.partial(_merge_kernel, n_ctx_blocks=n_cb, blocks_per_lat=lat_seq // tb, alpha=alpha)
    return pl.pallas_call(
        kern,
        grid=(n_tok // tb,),
        in_specs=[tok(D_MODEL), mg(0), mg(1), mg(2), ctx, lat, ctx, lat, ctx, lat,
                  full(wa), full(wc), full(wg), full(wo), full(mod_l), full(lng), full(lnb)],
        out_specs=[tok(D_MODEL), tok(D_MODEL)],
        out_shape=[jax.ShapeDtypeStruct((n_tok, D_MODEL), F32), jax.ShapeDtypeStruct((n_tok, D_MODEL), BF16)],
        compiler_params=_cparams("arbitrary"),
    )(x, proj, proj, proj, *branches, wa, wc, wg, wo, mod_l, lng, lnb)


def _top16(s):
    rowi = lax.broadcasted_iota(jnp.int32, (P_TOPK, 1), 0)
    out = jnp.zeros((P_TOPK, s.shape[1]), F32)
    w = s
    for it in range(P_TOPK):
        m = jnp.max(w, axis=0, keepdims=True)
        out = jnp.where(rowi == it, m, out)
        if it + 1 < P_TOPK:
            w = jnp.where(w >= m, -jnp.inf, w)
    return out


def _peer_select(s1, s2):
    sv1 = _top16(s1)
    sv2 = _top16(s2)
    m1 = sv1[0:1]
    m2 = sv2[0:1]
    ea = jnp.exp(sv1 - m1)
    eb = jnp.exp(sv2 - m2)
    cands = [sv1 + sv2[0:1]]
    wts = [ea * eb[0:1]]
    for b in range(1, SUBLANES):
        cands.append(sv1[0:SUBLANES] + sv2[b:b + 1])
        wts.append(ea[0:SUBLANES] * eb[b:b + 1])
    cands.append(sv2[SUBLANES:] + sv1[0:1])
    wts.append(eb[SUBLANES:] * ea[0:1])
    work = list(cands)
    tau = None
    for it in range(P_TOPK):
        m = functools.reduce(jnp.maximum, [jnp.max(w, axis=0, keepdims=True) for w in work])
        if it + 1 < P_TOPK:
            work = [jnp.where(w >= m, -jnp.inf, w) for w in work]
        tau = m
    z = functools.reduce(lambda a, b: a + b,
                         [jnp.sum(jnp.where(cd >= tau, wt, 0.0), axis=0, keepdims=True)
                          for cd, wt in zip(cands, wts)])
    e1 = jnp.exp(s1 - m1) * (1.0 / z)
    e2 = jnp.exp(s2 - m2)
    return tau, e1, e2


def _peer_kernel(h2_ref, x1_ref, wqt_ref, keys_ref, u_ref, v_ref, mod_ref, lng_ref, lnb_ref, o_ref,
                 qt_ref, s_ref, e_ref, tau_ref, ga_ref, acc_ref, *, n_ctx_blocks, blocks_per_lat, alpha):
    j = pl.program_id(1)
    last = pl.num_programs(1) - 1
    row = _mod_row(pl.program_id(0), n_ctx_blocks, blocks_per_lat)
    tb = TB_PEER
    n_col = tb // LANES

    @pl.when(j == 0)
    def _():
        qt_ref[...] = _mm_nt(wqt_ref[...], h2_ref[...])
        for hh in range(P_HEADS):
            for p in range(2):
                r0 = (hh * 2 + p) * N_KEYS
                s_ref[hh, p] = _mm(keys_ref[p], qt_ref[r0:r0 + N_KEYS, :])
        for cc in range(n_col):
            cs = slice(cc * LANES, (cc + 1) * LANES)
            for hh in range(P_HEADS):
                tau, e1, e2 = _peer_select(s_ref[hh, 0, :, cs], s_ref[hh, 1, :, cs])
                tau_ref[hh:hh + 1, cs] = tau
                e_ref[hh, 0, :, cs] = e1
                e_ref[hh, 1, :, cs] = e2
        acc_ref[...] = jnp.zeros_like(acc_ref)

    act = _mm_nt(u_ref[...], h2_ref[...])
    gel = 0.5 * act * (1.0 + lax.erf(act * (2.0 ** -0.5)))
    i1_base = pl.multiple_of(j * SUBLANES, SUBLANES)
    for cc in range(n_col):
        cs = slice(cc * LANES, (cc + 1) * LANES)
        for half in range(EB_PEER // N_KEYS):
            g = jnp.zeros((N_KEYS, LANES), F32)
            for hh in range(P_HEADS):
                s1row = s_ref[hh, 0, pl.ds(i1_base, SUBLANES), cs][half:half + 1]
                e1row = e_ref[hh, 0, pl.ds(i1_base, SUBLANES), cs][half:half + 1]
                sel = (s1row + s_ref[hh, 1, :, cs]) >= tau_ref[hh:hh + 1, cs]
                g = g + jnp.where(sel, e1row * e_ref[hh, 1, :, cs], 0.0)
            ga_ref[half * N_KEYS:(half + 1) * N_KEYS, cs] = (
                g * gel[half * N_KEYS:(half + 1) * N_KEYS, cs]).astype(BF16)
    acc_ref[...] += lax.dot_general(ga_ref[...], v_ref[...], (((0,), (0,)), ((), ())),
                                    preferred_element_type=F32)

    @pl.when(j == last)
    def _():
        g2 =mod_ref[pl.ds(row, 1), 5 * D_MODEL:6 * D_MODEL]
        o_ref[...] = _layer_norm(alpha * x1_ref[...] + g2 * acc_ref[...], lng_ref[...], lnb_ref[...])


def _peer_call(h2, x1, wqt, keys, u_tab, v_tab, mod_l, lng, lnb, n_ctx_tok, lat_seq, alpha):
    n_tok = h2.shape[0]
    tb = TB_PEER
    n_exp = u_tab.shape[0]
    tok = pl.BlockSpec((tb, D_MODEL), lambda i, j: (i, 0))
    full = lambda a: pl.BlockSpec(a.shape, lambda i, j: (0,) * a.ndim)
    tab = pl.BlockSpec((EB_PEER, D_MODEL), lambda i, j: (j, 0))
    kern = functools.partial(_peer_kernel, n_ctx_blocks=n_ctx_tok // tb, blocks_per_lat=lat_seq // tb,
                             alpha=alpha)
    return pl.pallas_call(
        kern,
        grid=(n_tok // tb, n_exp // EB_PEER),
        in_specs=[tok, tok, full(wqt), full(keys), tab, tab, full(mod_l), full(lng), full(lnb)],
        out_specs=tok,
        out_shape=jax.ShapeDtypeStruct((n_tok, D_MODEL), F32),
        scratch_shapes=[pltpu.VMEM((P_HEADS * 2 * N_KEYS, tb), F32),
                        pltpu.VMEM((P_HEADS, 2, N_KEYS, tb), F32),
                        pltpu.VMEM((P_HEADS, 2, N_KEYS, tb), F32),
                        pltpu.VMEM((P_HEADS, tb), F32),
                        pltpu.VMEM((EB_PEER, tb), BF16),
                        pltpu.VMEM((tb, D_MODEL), F32)],
        compiler_params=_cparams("arbitrary", "arbitrary"),
    )(h2, x1, wqt, keys, u_tab, v_tab, mod_l, lng, lnb)


def _rope_tables(n_tokens):
    n_rows = n_tokens // GRID_W
    rows = np.repeat(np.arange(n_rows, dtype=np.float32), GRID_W)
    cols = np.tile(np.arange(GRID_W, dtype=np.float32), n_rows)
    half = A_DQK // 2
    inv_freq = (1.0 / (ROPE_BASE ** (jnp.arange(0, half, 2, dtype=F32) / half)))
    ang_r = jnp.asarray(rows)[:, None] * inv_freq
    ang_c = jnp.asarray(cols)[:, None] * inv_freq
    zeros = jnp.zeros_like(ang_r)
    cr, sr, cc, sc = jnp.cos(ang_r), jnp.sin(ang_r), jnp.cos(ang_c), jnp.sin(ang_c)
    cos = jnp.concatenate([cr, cr, cc, cc], axis=-1)
    s_lo = jnp.concatenate([-sr, zeros, -sc, zeros], axis=-1)
    s_hi = jnp.concatenate([zeros, sr, zeros, sc], axis=-1)
    reps = 2 * A_HEADS
    return tuple(jnp.tile(t, (1, reps)) for t in (cos, s_lo, s_hi))


def _lane_vec(a):
    flat = a.reshape(1, -1).astype(F32)
    return jnp.pad(flat, ((0, 0), (0, LANES - flat.shape[1])))


def kernel(x_prompt, x_sample, cache_attn_k, cache_attn_v, state_gdn, c, c_ctx, w_mod, b_mod, w_in,
           diff_lambda, diff_norm_g, w_attn_o, conv_dw_w, conv_dw_b, conv_ln_g, conv_ln_b, w_conv_o,
           gdn_conv_w, gdn_A_log, gdn_dt_bias, gdn_norm_g, w_gdn_o, w_out, ln_g, ln_b,
           peer_wq, peer_keys, peer_u, peer_v):
    b_ctx, seq, d = x_prompt.shape
    b_lat, lat_seq, _ = x_sample.shape
    depth = w_mod.shape[0]
    past = cache_attn_k.shape[2]
    n_ctx_tok = b_ctx * seq
    alpha = (2 * depth) ** 0.25

    sizes = np.cumsum([0, 512, 512, 512, 1024, 1536, 512, 16, 3072])
    aq0, ak0, av0, cin0, gqkv0, gz0, gab0, mg0, end = [int(s) for s in sizes]
    w_p = jnp.concatenate([
        w_in[:, :, mg0:end], w_in[:, :, cin0:gqkv0], w_in[:, :, aq0:cin0], w_in[:, :, gz0:gab0],
        w_in[:, :, gqkv0:gz0], w_in[:, :, gab0:mg0],
        jnp.zeros((depth, d, N_PROJ - int(end)), w_in.dtype)], axis=-1).astype(BF16)
    wa_b, wc_b, wg_b, wo_b = (w.astype(BF16) for w in (w_attn_o, w_conv_o, w_gdn_o, w_out))
    wqt_b = jnp.swapaxes(peer_wq, 1, 2).astype(BF16)
    keys_b = peer_keys.astype(BF16)
    u_b = peer_u.astype(BF16)
    v_b = peer_v.astype(BF16)
    conv_w_p = jnp.pad(conv_dw_w, ((0, 0), (0, 32 - CONV_W), (0, 0)))
    rope_tabs = _rope_tables(lat_seq)
    cache_k = cache_attn_k.reshape(b_lat, depth, past, A_HEADS * LANES)
    cache_v = cache_attn_v.reshape(b_lat, depth, past, A_HEADS * LANES)

    cond8 = jnp.concatenate([c_ctx[None, :], c, jnp.zeros((SUBLANES - 1 - b_lat, d), F32)], axis=0)
    mod_all = _mod_call(cond8, w_mod, b_mod)

    x = jnp.concatenate([x_prompt.reshape(n_ctx_tok, d), x_sample.reshape(b_lat * lat_seq, d)], axis=0)
    k_list, v_list, s_list = [], [], []
    for l in range(depth):
        mod_l = mod_all[l]
        lam_init = 0.8 - 0.6 * math.exp(-0.3 * l)
        proj = _inproj_call(x, mod_l, w_p[l], n_ctx_tok, lat_seq)
        k_list.append(proj[:n_ctx_tok, OFF_AK:OFF_AK + 512].reshape(b_ctx, seq, A_HEADS, 2 * A_DQK))
        v_list.append(proj[:n_ctx_tok, OFF_AV:OFF_AV + 512].reshape(b_ctx, seq, A_HEADS, A_DV))

        dl = diff_lambda[l]
        ng_a = diff_norm_g[l].reshape(1, LANES)
        oa_ctx = _attn_ctx_call(proj, dl, ng_a, b_ctx, seq, lam_init)
        oa_lat = _attn_lat_call(proj, cache_k, cache_v, l, rope_tabs, dl, ng_a, n_ctx_tok, b_lat, lat_seq,
                                lam_init)

        cvec = lambda a: a[l].reshape(1, C_CONV)
        cargs = (conv_w_p[l], cvec(conv_dw_b), cvec(conv_ln_g), cvec(conv_ln_b))
        oc_ctx = _conv_call(proj, *cargs, 0, b_ctx, seq)
        oc_lat = _conv_call(proj, *cargs, n_ctx_tok // lat_seq, b_lat, lat_seq)

        gargs = (gdn_conv_w[l], _lane_vec(gdn_A_log[l]), _lane_vec(gdn_dt_bias[l]),
                 gdn_norm_g[l].reshape(1, LANES))
        og_ctx, s_new = _gdn_call(proj, *gargs, None, 0, b_ctx, seq, True)
        og_lat, _ = _gdn_call(proj, *gargs, state_gdn[:, l], n_ctx_tok // lat_seq, b_lat, lat_seq, False)
        s_list.append(s_new)

        branches = (oa_ctx, oa_lat, oc_ctx, oc_lat, og_ctx, og_lat)
        x1, h2 = _merge_call(x, proj, branches, wa_b[l], wc_b[l], wg_b[l], wo_b[l], mod_l,
                             ln_g[l, 0:1], ln_b[l, 0:1], n_ctx_tok, lat_seq, alpha)
        x = _peer_call(h2, x1, wqt_b[l], keys_b[l], u_b[l], v_b[l], mod_l, ln_g[l, 1:2], ln_b[l, 1:2],
                       n_ctx_tok, lat_seq, alpha)

    y_prompt = x[:n_ctx_tok].reshape(b_ctx, seq, d)
    y_sample = x[n_ctx_tok:].reshape(b_lat, lat_seq, d)
    return (y_prompt, y_sample, jnp.stack(k_list, axis=1), jnp.stack(v_list, axis=1),
            jnp.stack(s_list, axis=1))
```

```python
import functools
import math

import numpy as np
import jax
import jax.numpy as jnp
from jax import lax
from jax.experimental import pallas as pl
from jax.experimental.pallas import tpu as pltpu

F32 = jnp.float32
BF16 = jnp.bfloat16

LANES = 128
SUBLANES = 8
VMEM_LIMIT = 56 * 1024 * 1024

D_MODEL = 1024
A_HEADS = 4
A_DQK = 64
A_DV = 128
GRID_W = 64
ROPE_BASE = 10000.0
C_CONV = 512
CONV_W = 31
G_HEADS = 4
G_DK = 128
G_SHORT = 3
G_CHUNK = 128
P_HEADS = 8
N_KEYS = 128
P_TOPK = 16
LN_EPS = 1e-5

N_PROJ = 8192
OFF_MG, OFF_CA, OFF_CB, OFF_AQ, OFF_AK, OFF_AV, OFF_GZ, OFF_GQ, OFF_GK, OFF_GV, OFF_GAB = (
    0, 3072, 3584, 4096, 4608, 5120, 5632, 6144, 6656, 7168, 7680)

TB_PROJ = 512
TN_PROJ = 2048
TB_MERGE = 256
TB_PEER = 512
EB_PEER = 1024


def _cparams(*sem):
    return pltpu.CompilerParams(dimension_semantics=sem, vmem_limit_bytes=VMEM_LIMIT)


def _mm(a, b):
    return jnp.dot(a.astype(BF16), b.astype(BF16), preferred_element_type=F32)


def _mm_nt(a, b):
    return lax.dot_general(a.astype(BF16), b.astype(BF16), (((1,), (1,)), ((), ())),
                           preferred_element_type=F32)


def _silu(x):
    return x * jax.nn.sigmoid(x)


def _mod_row(i, n_ctx_blocks, blocks_per_lat):
    return jnp.where(i < n_ctx_blocks, 0, 1 + (i - n_ctx_blocks) // blocks_per_lat)


def _layer_norm(x, g, b):
    mu = jnp.mean(x, axis=-1, keepdims=True)
    xc = x - mu
    var = jnp.mean(xc * xc, axis=-1, keepdims=True)
    return xc * lax.rsqrt(var + LN_EPS) * g + b


def _mod_kernel(cond_ref, w_ref, b_ref, o_ref):
    a = _silu(cond_ref[...])
    o_ref[0] = _mm(a, w_ref[0]) + b_ref[0]


def _mod_call(cond8, w_mod, b_mod):
    depth, d, n = w_mod.shape
    tn = 1536
    return pl.pallas_call(
        _mod_kernel,
        grid=(depth, n // tn),
        in_specs=[pl.BlockSpec((SUBLANES, d), lambda l, j: (0, 0)),
                  pl.BlockSpec((1, d, tn), lambda l, j: (l, 0, j)),
                  pl.BlockSpec((1, 1, tn), lambda l, j: (l, 0, j))],
        out_specs=pl.BlockSpec((1, SUBLANES, tn), lambda l, j: (l, 0, j)),
        out_shape=jax.ShapeDtypeStruct((depth, SUBLANES, n), F32),
        compiler_params=_cparams("arbitrary", "arbitrary"),
    )(cond8, w_mod, b_mod.reshape(depth, 1, n))


def _inproj_kernel(x_ref, mod_ref, w_ref, o_ref, *, n_ctx_blocks, blocks_per_lat):
    row = _mod_row(pl.program_id(1), n_ctx_blocks, blocks_per_lat)
    sh = mod_ref[pl.ds(row, 1), 0:D_MODEL]
    sc = mod_ref[pl.ds(row, 1), D_MODEL:2 * D_MODEL]
    h = x_ref[...] * (1.0 + sc) + sh
    o_ref[...] = _mm(h, w_ref[...])


def _inproj_call(x, mod_l, w_p, n_ctx_tok, lat_seq):
    n_tok = x.shape[0]
    kern = functools.partial(_inproj_kernel, n_ctx_blocks=n_ctx_tok // TB_PROJ,
                             blocks_per_lat=lat_seq // TB_PROJ)
    return pl.pallas_call(
        kern,
        grid=(N_PROJ // TN_PROJ, n_tok // TB_PROJ),
        in_specs=[pl.BlockSpec((TB_PROJ, D_MODEL), lambda j, i: (i, 0)),
                  pl.BlockSpec((SUBLANES, 6 * D_MODEL), lambda j, i: (0, 0)),
                  pl.BlockSpec((D_MODEL, TN_PROJ), lambda j, i: (0, j))],
        out_specs=pl.BlockSpec((TB_PROJ, TN_PROJ), lambda j, i: (i, j)),
        out_shape=jax.ShapeDtypeStruct((n_tok, N_PROJ), F32),
        compiler_params=_cparams("arbitrary", "arbitrary"),
    )(x, mod_l, w_p)


def _softmax(s):
    e = jnp.exp(s - jnp.max(s, axis=-1, keepdims=True))
    return e / jnp.sum(e, axis=-1, keepdims=True)


def _diff_lambda(dl_ref, lam_init):
    dl = dl_ref[...]
    a = jnp.sum(dl[0:1] * dl[1:2], axis=-1, keepdims=True)
    b = jnp.sum(dl[2:3] * dl[3:4], axis=-1, keepdims=True)
    return jnp.exp(a) - jnp.exp(b) + lam_init


def _diff_attn_heads(q, k_of, v_of, lam, ng, lam_init, o_ref):
    lo = lax.broadcasted_iota(jnp.int32, (1, LANES), 1) < A_DQK
    scale = A_DQK ** -0.5
    for h in range(A_HEADS):
        qh = q[:, h * LANES:(h + 1) * LANES]
        kh = k_of(h)
        s1 = _mm_nt(jnp.where(lo, qh, 0.0), kh) * scale
        s2 = _mm_nt(jnp.where(lo, 0.0, qh), kh) * scale
        a = _softmax(s1) - lam * _softmax(s2)
        o = _mm(a, v_of(h))
        o = o * lax.rsqrt(jnp.mean(o * o, axis=-1, keepdims=True) + LN_EPS) * ng
        o_ref[:, h * LANES:(h + 1) * LANES] = (o * (1.0 - lam_init)).astype(o_ref.dtype)


def _attn_ctx_kernel(q_ref, k_ref, v_ref, dl_ref, ng_ref, o_ref, *, lam_init):
    lam = _diff_lambda(dl_ref, lam_init)
    _diff_attn_heads(q_ref[...],
                     lambda h: k_ref[:, h * LANES:(h + 1) * LANES].astype(BF16),
                     lambda h: v_ref[:, h * LANES:(h + 1) * LANES].astype(BF16),
                     lam, ng_ref[...], lam_init, o_ref)


def _attn_ctx_call(proj, dl, ng, n_seq, seq, lam_init):
    w = A_HEADS * LANES
    spec = lambda off: pl.BlockSpec((seq, w), lambda b, off=off: (b, off // w))
    return pl.pallas_call(
        functools.partial(_attn_ctx_kernel, lam_init=lam_init),
        grid=(n_seq,),
        in_specs=[spec(OFF_AQ), spec(OFF_AK), spec(OFF_AV),
                  pl.BlockSpec(dl.shape, lambda b: (0, 0)),
                  pl.BlockSpec((1, LANES), lambda b: (0, 0))],
        out_specs=pl.BlockSpec((seq, w), lambda b: (b, 0)),
        out_shape=jax.ShapeDtypeStruct((n_seq * seq, w), BF16),
        compiler_params=_cparams("arbitrary"),
    )(proj, proj, proj, dl, ng)


def _rope(x, cos, sin_lo, sin_hi):
    n = x.shape[-1]
    return x * cos + pltpu.roll(x, n - 16, 1) * sin_lo + pltpu.roll(x, 16, 1) * sin_hi


def _attn_lat_kernel(q_ref, k_ref, v_ref, ck_ref, cv_ref, cosq_ref, slq_ref, shq_ref,
                     cosk_ref, slk_ref, shk_ref, dl_ref, ng_ref, o_ref, kall_ref, vall_ref,
                     *, lam_init, past):
    @pl.when(pl.program_id(1) == 0)
    def _():
        kall_ref[0:past, :] = ck_ref[0, 0].astype(BF16)
        vall_ref[0:past, :] = cv_ref[0, 0].astype(BF16)
        kall_ref[past:, :] = _rope(k_ref[...], cosk_ref[...], slk_ref[...], shk_ref[...]).astype(BF16)
        vall_ref[past:, :] = v_ref[...].astype(BF16)

    lam = _diff_lambda(dl_ref, lam_init)
    q = _rope(q_ref[...], cosq_ref[...], slq_ref[...], shq_ref[...])
    _diff_attn_heads(q,
                     lambda h: kall_ref[:, h * LANES:(h + 1) * LANES],
                     lambda h: vall_ref[:, h * LANES:(h + 1) * LANES],
                     lam, ng_ref[...], lam_init, o_ref)


def _attn_lat_call(proj, cache_k, cache_v, layer, rope_tabs, dl, ng, n_ctx_tok, n_seq, seq, lam_init):
    w = A_HEADS * LANES
    past = cache_k.shape[2]
    qb = 256
    nqb = seq // qb
    row0 = n_ctx_tok // seq
    rowq0 = n_ctx_tok // qb
    cos, s_lo, s_hi = rope_tabs
    qspec = pl.BlockSpec((qb, w), lambda b, i: (rowq0 + b * nqb + i, OFF_AQ // w))
    kspec = pl.BlockSpec((seq, w), lambda b, i: (row0 + b, OFF_AK // w))
    vspec = pl.BlockSpec((seq, w), lambda b, i: (row0 + b, OFF_AV // w))
    cspec = pl.BlockSpec((1, 1, past, w), lambda b, i: (b, layer, 0, 0))
    tq = pl.BlockSpec((qb, w), lambda b, i: (i, 0))
    tk = pl.BlockSpec((seq, w), lambda b, i: (0, 0))
    return pl.pallas_call(
        functools.partial(_attn_lat_kernel, lam_init=lam_init, past=past),
        grid=(n_seq, nqb),
        in_specs=[qspec, kspec, vspec, cspec, cspec, tq, tq, tq, tk, tk, tk,
                  pl.BlockSpec(dl.shape, lambda b, i: (0, 0)),
                  pl.BlockSpec((1, LANES), lambda b, i: (0, 0))],
        out_specs=pl.BlockSpec((qb, w), lambda b, i: (b * nqb + i, 0)),
        out_shape=jax.ShapeDtypeStruct((n_seq * seq, w), BF16),
        scratch_shapes=[pltpu.VMEM((past + seq, w), BF16), pltpu.VMEM((past + seq, w), BF16)],
        compiler_params=_cparams("arbitrary", "arbitrary"),
    )(proj, proj, proj, cache_k, cache_v, cos, s_lo, s_hi, cos, s_lo, s_hi, dl, ng)


CONV_PAD = 16
CONV_ROWS = 64


def _conv_kernel(ca_ref, cb_ref, w_ref, b_ref, lg_ref, lb_ref, o_ref, pad_ref, *, seq):
    zeros = jnp.zeros((CONV_PAD, C_CONV), F32)
    pad_ref[0:CONV_PAD, :] = zeros
    pad_ref[seq + CONV_PAD:seq + 2 * CONV_PAD, :] = zeros
    pad_ref[CONV_PAD:seq + CONV_PAD, :] = ca_ref[...] * jax.nn.sigmoid(cb_ref[...])
    half = CONV_W // 2

    def body(ci, carry):
        base = pl.multiple_of(ci * CONV_ROWS, CONV_ROWS)
        win = pad_ref[pl.ds(base, CONV_ROWS + 2 * CONV_PAD), :]
        acc = jnp.zeros((CONV_ROWS, C_CONV), F32)
        for r in range(SUBLANES):
            shifted = win[r:r + CONV_ROWS + 3 * SUBLANES]
            for qq in range(4):
                j = qq * SUBLANES + r - (CONV_PAD - half)
                if 0 <= j < CONV_W:
                    acc = acc + w_ref[j:j + 1, :] * shifted[qq * SUBLANES:qq * SUBLANES + CONV_ROWS]
        y = _layer_norm(acc + b_ref[...], lg_ref[...], lb_ref[...])
        o_ref[pl.ds(base, CONV_ROWS), :] = _silu(y).astype(o_ref.dtype)
        return carry

    lax.fori_loop(0, seq // CONV_ROWS, body, 0)


def _conv_call(proj, w, b, lg, lb, row0, n_seq, seq):
    spec = lambda off: pl.BlockSpec((seq, C_CONV), lambda s, off=off: (row0 + s, off // C_CONV))
    vec = pl.BlockSpec((1, C_CONV), lambda s: (0, 0))
    return pl.pallas_call(
        functools.partial(_conv_kernel, seq=seq),
        grid=(n_seq,),
        in_specs=[spec(OFF_CA), spec(OFF_CB), pl.BlockSpec(w.shape, lambda s: (0, 0)), vec, vec, vec],
        out_specs=pl.BlockSpec((seq, C_CONV), lambda s: (s, 0)),
        out_shape=jax.ShapeDtypeStruct((n_seq * seq, C_CONV), BF16),
        scratch_shapes=[pltpu.VMEM((seq + 2 * CONV_PAD, C_CONV), F32)],
        compiler_params=_cparams("arbitrary"),
    )(proj, proj, w, b, lg, lb)


GDN_GROUP = 4


def _split(a):
    hi = a.astype(BF16)
    return hi, (a - hi.astype(F32)).astype(BF16)


def _mm_split(a, b):
    (ah, al), (bh, bl) = a, b
    dot = lambda x, y: jnp.dot(x, y, preferred_element_type=F32)
    return dot(ah, bh) + (dot(ah, bl) + dot(al, bh))


def _unit_tri_inverses(l_mats):
    n = l_mats[0].shape[0]
    ri = lax.broadcasted_iota(jnp.int32, (n, n), 0)
    ci = lax.broadcasted_iota(jnp.int32, (n, n), 1)
    same_block = lambda size: (ri // size) == (ci // size)
    eye = (ri == ci).astype(F32)
    diag = [jnp.where(same_block(SUBLANES), l, 0.0) for l in l_mats]
    ps = [eye - d for d in diag]
    ms = [_split(d) for d in diag]
    for _ in range(int(math.log2(SUBLANES)) - 1):
        ms = [_split(_mm_split(m, m)) for m in ms]
        ps = [p + _mm_split(_split(p), m) for p, m in zip(ps, ms)]
    size = SUBLANES
    while size < n:
        off = same_block(2 * size) & jnp.logical_not(same_block(size))
        new_ps = []
        for p, l in zip(ps, l_mats):
            p_s = _split(p)
            t = _mm_split(_split(jnp.where(off, l, 0.0)), p_s)
            new_ps.append(p - _mm_split(p_s, _split(t)))
        ps = new_ps
        size *= 2
    return ps


def _gdn_kernel(*refs, seq, has_s0, emit_state):
    (q_ref, k_ref, v_ref, gab_ref, z_ref, cwq_ref, cwk_ref, cwv_ref, al_ref, dt_ref, ng_ref) = refs[:11]
    idx = 11
    s0_ref = None
    if has_s0:
        s0_ref = refs[idx]
        idx += 1
    o_ref = refs[idx]
    idx += 1
    sn_ref = None
    if emit_state:
        sn_ref = refs[idx]
        idx += 1
    (pad_ref, kc_s, kt_s, qc_s, vc_s, gcb_s, btb_s, grow_s, w_s, u_s, qk_s, cumt_ref, oacc_s) = refs[idx:]

    c = G_CHUNK
    n_chunks = seq // c
    head = pl.program_id(1)

    zeros8 = jnp.zeros((SUBLANES, LANES), F32)
    pad_ref[0:SUBLANES, :] = zeros8
    pad_ref[seq + SUBLANES:seq + 2 * SUBLANES, :] = zeros8

    def short_conv(x_ref, w_ref):
        pad_ref[SUBLANES:seq + SUBLANES, :] = x_ref[...]
        y = (w_ref[0:1, :] * pad_ref[SUBLANES - 1:seq + SUBLANES - 1, :]
             + w_ref[1:2, :] * pad_ref[SUBLANES:seq + SUBLANES, :]
             + w_ref[2:3, :] * pad_ref[SUBLANES + 1:seq + SUBLANES + 1, :])
        return _silu(y)

    q = short_conv(q_ref, cwq_ref)
    k = short_conv(k_ref, cwk_ref)
    v = short_conv(v_ref, cwv_ref)
    qn = q * lax.rsqrt(jnp.sum(q * q, axis=-1, keepdims=True) + 1e-6) * (G_DK ** -0.5)
    kn = k * lax.rsqrt(jnp.sum(k * k, axis=-1, keepdims=True) + 1e-6)

    gab = gab_ref[...]
    xg = gab + dt_ref[...]
    softplus = jnp.maximum(xg, 0.0) + jnp.log1p(jnp.exp(-jnp.abs(xg)))
    g_all = -jnp.exp(al_ref[...]) * softplus
    b_all = jax.nn.sigmoid(gab)

    g_t = g_all.T
    pos = lax.broadcasted_iota(jnp.int32, (1, seq), 1) % c
    yf = g_t
    yb = g_t
    s = 1
    while s < c:
        yf = yf + jnp.where(pos >= s, pltpu.roll(yf, s, 1), 0.0)
        yb = yb + jnp.where(pos < c - s, pltpu.roll(yb, seq - s, 1), 0.0)
        s *= 2
    rowi = lax.broadcasted_iota(jnp.int32, (LANES, 1), 0)
    cum_t = jnp.where(rowi < G_HEADS, yf, yb)
    cumt_ref[...] = cum_t
    cum_c = cum_t.T
    lane = lax.broadcasted_iota(jnp.int32, (1, LANES), 1)

    for d in range(2):
        r = d * G_HEADS + head
        gcol = jnp.sum(jnp.where(lane == r, cum_c, 0.0), axis=1, keepdims=True)
        bcol = jnp.sum(jnp.where(lane == 2 * G_HEADS + r, b_all, 0.0), axis=1, keepdims=True)
        grow = cumt_ref[pl.ds(r, 1), :]
        for n in range(n_chunks):
            sl = slice(n * c, (n + 1) * c)
            gcb_s[d, n] = jnp.broadcast_to(gcol[sl], (c, LANES))
            btb_s[d, n] = jnp.broadcast_to(bcol[sl], (c, LANES))
            grow_s[d, n] = jnp.broadcast_to(grow[:, sl], (SUBLANES, c))
    for n in range(n_chunks):
        sl = slice(n * c, (n + 1) * c)
        kc_s[n] = kn[sl]
        kt_s[n] = kn[sl].T
        qc_s[n] = qn[sl]
        vc_s[n] = v[sl]

    ri = lax.broadcasted_iota(jnp.int32, (c, c), 0)
    ci = lax.broadcasted_iota(jnp.int32, (c, c), 1)

    incl = [ri >= ci, ri <= ci]
    strict = [ri > ci, ri < ci]

    instances = [(d, n) for n in range(n_chunks) for d in range(2)]
    for g0 in range(0, len(instances), GDN_GROUP):
        group = instances[g0:g0 + GDN_GROUP]
        l_mats, rhs_w, rhs_u = [], [], []
        for d, n in group:
            kt = kt_s[n]
            gcb = gcb_s[d, n]
            btb = btb_s[d, n]
            decay = jnp.exp(jnp.where(incl[d], gcb - grow_s[d, n][0:1, :], -jnp.inf))
            kbeta = kc_s[n] * btb
            l_mats.append(jnp.where(strict[d], _mm(kbeta, kt) * decay, 0.0))
            rhs_w.append(_split(kbeta * jnp.exp(gcb)))
            rhs_u.append(_split(vc_s[n] * btb))
            qk_s[d, n] = jnp.where(incl[d], _mm(qc_s[n], kt) * decay, 0.0)
        invs = [_split(p) for p in _unit_tri_inverses(l_mats)]
        for (d, n), inv, rw, ru in zip(group, invs, rhs_w, rhs_u):
            w_s[d, n] = _mm_split(inv, rw)
            u_s[d, n] = _mm_split(inv, ru)

    states = [s0_ref[0, d, 0] if has_s0 else jnp.zeros((G_DK, LANES), F32) for d in range(2)]
    for i in range(n_chunks):
        for d in range(2):
            n = i if d == 0 else n_chunks - 1 - i
            gcb = gcb_s[d, n]
            glast = gcb[c - 1:c, :] if d == 0 else gcb[0:1, :]
            v_new = u_s[d, n] - _mm(w_s[d, n], states[d])
            oacc_s[d, n] = _mm(qc_s[n] * jnp.exp(gcb), states[d]) + _mm(qk_s[d, n], v_new)
            kdec_t = kt_s[n] * jnp.exp(glast - grow_s[d, n][0:1, :])
            states[d] = states[d] * jnp.exp(glast) + _mm(kdec_t, v_new)
    if emit_state:
        for d in range(2):
            sn_ref[0, d, 0] = states[d]

    ng = ng_ref[...]
    for n in range(n_chunks):
        sl = slice(n * c, (n + 1) * c)
        o = oacc_s[0, n] + oacc_s[1, n]
        o = o * lax.rsqrt(jnp.mean(o * o, axis=-1, keepdims=True) + LN_EPS) * ng
        o_ref[sl, :] = (o * _silu(z_ref[sl, :])).astype(o_ref.dtype)


def _gdn_call(proj, cw, al, dt, ng, s0, row0, n_seq, seq, emit_state):
    n_chunks = seq // G_CHUNK
    col = lambda off: pl.BlockSpec((seq, LANES), lambda b, h, off=off: (row0 + b, off // LANES + h))
    cwspec = lambda part: pl.BlockSpec((G_SHORT, LANES), lambda b, h, part=part: (0, part * G_HEADS + h))
    vec = pl.BlockSpec((1, LANES), lambda b, h: (0, 0))
    in_specs = [col(OFF_GQ), col(OFF_GK), col(OFF_GV),
                pl.BlockSpec((seq, LANES), lambda b, h: (row0 + b, OFF_GAB // LANES)),
                col(OFF_GZ), cwspec(0), cwspec(1), cwspec(2), vec, vec, vec]
    args = [proj, proj, proj, proj, proj, cw, cw, cw, al, dt, ng]
    state_spec = pl.BlockSpec((1, 2, 1, G_DK, LANES), lambda b, h: (b, 0, h, 0, 0))
    if s0 is not None:
        in_specs.append(state_spec)
        args.append(s0)
    out_specs = [pl.BlockSpec((seq, LANES), lambda b, h: (b, h))]
    out_shape = [jax.ShapeDtypeStruct((n_seq * seq, G_HEADS * LANES), BF16)]
    if emit_state:
        out_specs.append(state_spec)
        out_shape.append(jax.ShapeDtypeStruct((n_seq, 2, G_HEADS, G_DK, LANES), F32))
    chunked = lambda lead: pltpu.VMEM(lead + (G_CHUNK, LANES), F32)
    scratch = [pltpu.VMEM((seq + 2 * SUBLANES, LANES), F32),
               chunked((n_chunks,)), chunked((n_chunks,)), chunked((n_chunks,)), chunked((n_chunks,)),
               chunked((2, n_chunks)), chunked((2, n_chunks)),
               pltpu.VMEM((2, n_chunks, SUBLANES, G_CHUNK), F32),
               chunked((2, n_chunks)), chunked((2, n_chunks)), chunked((2, n_chunks)),
               pltpu.VMEM((LANES, seq), F32), chunked((2, n_chunks))]
    res = pl.pallas_call(
        functools.partial(_gdn_kernel, seq=seq, has_s0=s0 is not None, emit_state=emit_state),
        grid=(n_seq, G_HEADS),
        in_specs=in_specs,
        out_specs=out_specs,
        out_shape=out_shape,
        scratch_shapes=scratch,
        compiler_params=_cparams("arbitrary", "arbitrary"),
    )(*args)
    return res if emit_state else (res[0], None)


def _merge_kernel(x_ref, mg0_ref, mg1_ref, mg2_ref, oac_ref, oal_ref, occ_ref, ocl_ref, ogc_ref, ogl_ref,
                  wa_ref, wc_ref, wg_ref, wo_ref, mod_ref, lng_ref, lnb_ref, x1_ref, h2_ref,
                  *, n_ctx_blocks, blocks_per_lat, alpha):
    i = pl.program_id(0)
    row = _mod_row(i, n_ctx_blocks, blocks_per_lat)
    g1 = mod_ref[pl.ds(row, 1), 2 * D_MODEL:3 * D_MODEL]
    sh2 = mod_ref[pl.ds(row, 1), 3 * D_MODEL:4 * D_MODEL]
    sc2 = mod_ref[pl.ds(row, 1), 4 * D_MODEL:5 * D_MODEL]
    is_ctx = i < n_ctx_blocks
    pick = lambda c_ref, l_ref: jnp.where(is_ctx, c_ref[...], l_ref[...])
    merged = (jax.nn.sigmoid(mg0_ref[...]) * _mm(pick(oac_ref, oal_ref), wa_ref[...])
              + jax.nn.sigmoid(mg1_ref[...]) * _mm(pick(occ_ref, ocl_ref), wc_ref[...])
              + jax.nn.sigmoid(mg2_ref[...]) * _mm(pick(ogc_ref, ogl_ref), wg_ref[...]))
    mix = _mm(merged, wo_ref[...])
    x1 = _layer_norm(alpha * x_ref[...] + g1 * mix, lng_ref[...], lnb_ref[...])
    x1_ref[...] = x1
    h2_ref[...] = (x1 * (1.0 + sc2) + sh2).astype(h2_ref.dtype)


def _merge_call(x, proj, branches, wa, wc, wg, wo, mod_l, lng, lnb, n_ctx_tok, lat_seq, alpha):
    n_tok = x.shape[0]
    tb = TB_MERGE
    n_cb = n_ctx_tok // tb
    tok = lambda w: pl.BlockSpec((tb, w), lambda i: (i, 0))
    mg = lambda j: pl.BlockSpec((tb, D_MODEL), lambda i, j=j: (i, j))
    full = lambda a: pl.BlockSpec(a.shape, lambda i: (0, 0))
    ctx = pl.BlockSpec((tb, 512), lambda i: (jnp.minimum(i, n_cb - 1), 0))
    lat = pl.BlockSpec((tb, 512), lambda i: (jnp.maximum(i - n_cb, 0), 0))
    kern = functools.partial(_merge_kernel, n_ctx_blocks=n_cb, blocks_per_lat=lat_seq // tb, alpha=alpha)
    return pl.pallas_call(
        kern,
        grid=(n_tok // tb,),
        in_specs=[tok(D_MODEL), mg(0), mg(1), mg(2), ctx, lat, ctx, lat, ctx, lat,
                  full(wa), full(wc), full(wg), full(wo), full(mod_l), full(lng), full(lnb)],
        out_specs=[tok(D_MODEL), tok(D_MODEL)],
        out_shape=[jax.ShapeDtypeStruct((n_tok, D_MODEL), F32), jax.ShapeDtypeStruct((n_tok, D_MODEL), BF16)],
        compiler_params=_cparams("arbitrary"),
    )(x, proj, proj, proj, *branches, wa, wc, wg, wo, mod_l, lng, lnb)


def _top16(s):
    rowi = lax.broadcasted_iota(jnp.int32, (P_TOPK, 1), 0)
    out = jnp.zeros((P_TOPK, s.shape[1]), F32)
    w = s
    for it in range(P_TOPK):
        m = jnp.max(w, axis=0, keepdims=True)
        out = jnp.where(rowi == it, m, out)
        if it + 1 < P_TOPK:
            w = jnp.where(w >= m, -jnp.inf, w)
    return out


def _peer_select(s1, s2):
    sv1 = _top16(s1)
    sv2 = _top16(s2)
    m1 = sv1[0:1]
    m2 = sv2[0:1]
    ea = jnp.exp(sv1 - m1)
    eb = jnp.exp(sv2 - m2)
    lo, hi = slice(0, SUBLANES), slice(SUBLANES, P_TOPK)
    cands = [sv1[lo] + sv2[0:1], sv1[hi] + sv2[0:1]]
    wts = [ea[lo] * eb[0:1], ea[hi] * eb[0:1]]
    for b in range(1, SUBLANES):
        cands.append(sv1[lo] + sv2[b:b + 1])
        wts.append(ea[lo] * eb[b:b + 1])
    cands.append(sv2[hi] + sv1[0:1])
    wts.append(eb[hi] * ea[0:1])
    work = list(cands)
    tau = None
    for it in range(P_TOPK):
        tau = jnp.max(functools.reduce(jnp.maximum, work), axis=0, keepdims=True)
        if it + 1 < P_TOPK:
            work = [jnp.where(w >= tau, -jnp.inf, w) for w in work]
    z = jnp.sum(functools.reduce(lambda a, b: a + b,
                                 [jnp.where(cd >= tau, wt, 0.0) for cd, wt in zip(cands, wts)]),
                axis=0, keepdims=True)
    thr = jnp.full(s1.shape, jnp.inf, F32)
    for b in range(P_TOPK):
        thr = jnp.where(s1 + sv2[b:b + 1] >= tau, sv2[b:b + 1], thr)
    e1 = jnp.exp(s1 - m1) * (1.0 / z)
    e2 = jnp.exp(s2 - m2)
    return thr, e1, e2


def _peer_kernel(h2_ref, x1_ref, wqt_ref, keys_ref, u_ref, v_ref, mod_ref, lng_ref, lnb_ref, o_ref,
                 qt_ref, s_ref, e_ref, thr_ref, ga_ref, acc_ref, *, n_ctx_blocks, blocks_per_lat, alpha):
    j = pl.program_id(1)
    last = pl.num_programs(1) - 1
    row = _mod_row(pl.program_id(0), n_ctx_blocks, blocks_per_lat)
    tb = TB_PEER
    n_col = tb // LANES

    @pl.when(j == 0)
    def _():
        qt_ref[...] = _mm_nt(wqt_ref[...], h2_ref[...])
        for hh in range(P_HEADS):
            for p in range(2):
                r0 = (hh * 2 + p) * N_KEYS
                s_ref[hh, p] = _mm(keys_ref[p], qt_ref[r0:r0 + N_KEYS, :])
        for cc in range(n_col):
            cs = slice(cc * LANES, (cc + 1) * LANES)
            for hh in range(P_HEADS):
                thr, e1, e2 = _peer_select(s_ref[hh, 0, :, cs], s_ref[hh, 1, :, cs])
                thr_ref[hh, :, cs] = thr
                e_ref[hh, 0, :, cs] = e1
                e_ref[hh, 1, :, cs] = e2
        acc_ref[...] = jnp.zeros_like(acc_ref)

    act = _mm_nt(u_ref[...], h2_ref[...])
    gel = 0.5 * act * (1.0 + lax.erf(act * (2.0 ** -0.5)))
    i1_base = pl.multiple_of(j * SUBLANES, SUBLANES)
    for cc in range(n_col):
        cs = slice(cc * LANES, (cc + 1) * LANES)
        for half in range(EB_PEER // N_KEYS):
            g = jnp.zeros((N_KEYS, LANES), F32)
            for hh in range(P_HEADS):
                thr_row = thr_ref[hh, pl.ds(i1_base, SUBLANES), cs][half:half + 1]
                e1row = e_ref[hh, 0, pl.ds(i1_base, SUBLANES), cs][half:half + 1]
                sel = s_ref[hh, 1, :, cs] >= thr_row
                g = g + jnp.where(sel, e1row * e_ref[hh, 1, :, cs], 0.0)
            ga_ref[half * N_KEYS:(half + 1) * N_KEYS, cs] = (
                g * gel[half * N_KEYS:(half + 1) * N_KEYS, cs]).astype(BF16)
    acc_ref[...] += lax.dot_general(ga_ref[...], v_ref[...], (((0,), (0,)), ((), ())),
                                    preferred_element_type=F32)

    @pl.when(j == last)
    def _():
        g2 = mod_ref[pl.ds(row, 1), 5 * D_MODEL:6 * D_MODEL]
        o_ref[...] = _layer_norm(alpha * x1_ref[...] + g2 * acc_ref[...], lng_ref[...], lnb_ref[...])


def _peer_call(h2, x1, wqt, keys, u_tab, v_tab, mod_l, lng, lnb, n_ctx_tok, lat_seq, alpha):
    n_tok = h2.shape[0]
    tb = TB_PEER
    n_exp = u_tab.shape[0]
    tok = pl.BlockSpec((tb, D_MODEL), lambda i, j: (i, 0))
    full = lambda a: pl.BlockSpec(a.shape, lambda i, j: (0,) * a.ndim)
    tab = pl.BlockSpec((EB_PEER, D_MODEL), lambda i, j: (j, 0))
    kern = functools.partial(_peer_kernel, n_ctx_blocks=n_ctx_tok // tb, blocks_per_lat=lat_seq // tb,
                             alpha=alpha)
    return pl.pallas_call(
        kern,
        grid=(n_tok // tb, n_exp // EB_PEER),
        in_specs=[tok, tok, full(wqt), full(keys), tab, tab, full(mod_l), full(lng), full(lnb)],
        out_specs=tok,
        out_shape=jax.ShapeDtypeStruct((n_tok, D_MODEL), F32),
        scratch_shapes=[pltpu.VMEM((P_HEADS * 2 * N_KEYS, tb), F32),
                        pltpu.VMEM((P_HEADS, 2, N_KEYS, tb), F32),
                        pltpu.VMEM((P_HEADS, 2, N_KEYS, tb), F32),
                        pltpu.VMEM((P_HEADS, N_KEYS, tb), F32),
                        pltpu.VMEM((EB_PEER, tb), BF16),
                        pltpu.VMEM((tb, D_MODEL), F32)],
        compiler_params=_cparams("arbitrary", "arbitrary"),
    )(h2, x1, wqt, keys, u_tab, v_tab, mod_l, lng, lnb)


def _rope_tables(n_tokens):
    n_rows = n_tokens // GRID_W
    rows = np.repeat(np.arange(n_rows, dtype=np.float32), GRID_W)
    cols = np.tile(np.arange(GRID_W, dtype=np.float32), n_rows)
    half = A_DQK // 2
    inv_freq = (1.0 / (ROPE_BASE ** (jnp.arange(0, half, 2, dtype=F32) / half)))
    ang_r = jnp.asarray(rows)[:, None] * inv_freq
    ang_c = jnp.asarray(cols)[:, None] * inv_freq
    zeros = jnp.zeros_like(ang_r)
    cr, sr, cc, sc = jnp.cos(ang_r), jnp.sin(ang_r), jnp.cos(ang_c), jnp.sin(ang_c)
    cos = jnp.concatenate([cr, cr, cc, cc], axis=-1)
    s_lo = jnp.concatenate([-sr, zeros, -sc, zeros], axis=-1)
    s_hi = jnp.concatenate([zeros, sr, zeros, sc], axis=-1)
    reps = 2 * A_HEADS
    return tuple(jnp.tile(t, (1, reps)) for t in (cos, s_lo, s_hi))


def _lane_vec(a):
    flat = a.reshape(1, -1).astype(F32)
    return jnp.pad(flat, ((0, 0), (0, LANES - flat.shape[1])))


def kernel(x_prompt, x_sample, cache_attn_k, cache_attn_v, state_gdn, c, c_ctx, w_mod, b_mod, w_in,
           diff_lambda, diff_norm_g, w_attn_o, conv_dw_w, conv_dw_b, conv_ln_g, conv_ln_b, w_conv_o,
           gdn_conv_w, gdn_A_log, gdn_dt_bias, gdn_norm_g, w_gdn_o, w_out, ln_g, ln_b,
           peer_wq, peer_keys, peer_u, peer_v):
    b_ctx, seq, d = x_prompt.shape
    b_lat, lat_seq, _ = x_sample.shape
    depth = w_mod.shape[0]
    past = cache_attn_k.shape[2]
    n_ctx_tok = b_ctx * seq
    alpha = (2 * depth) ** 0.25

    sizes = np.cumsum([0, 512, 512, 512, 1024, 1536, 512, 16, 3072])
    aq0, ak0, av0, cin0, gqkv0, gz0, gab0, mg0, end = [int(s) for s in sizes]
    w_p = jnp.concatenate([
        w_in[:, :, mg0:end], w_in[:, :, cin0:gqkv0], w_in[:, :, aq0:cin0], w_in[:, :, gz0:gab0],
        w_in[:, :, gqkv0:gz0], w_in[:, :, gab0:mg0],
        jnp.zeros((depth, d, N_PROJ - int(end)), w_in.dtype)], axis=-1).astype(BF16)
    wa_b, wc_b, wg_b, wo_b = (w.astype(BF16) for w in (w_attn_o, w_conv_o, w_gdn_o, w_out))
    wqt_b = jnp.swapaxes(peer_wq, 1, 2).astype(BF16)
    keys_b = peer_keys.astype(BF16)
    u_b = peer_u.astype(BF16)
    v_b = peer_v.astype(BF16)
    conv_w_p = jnp.pad(conv_dw_w, ((0, 0), (0, 32 - CONV_W), (0, 0)))
    rope_tabs = _rope_tables(lat_seq)
    cache_k = cache_attn_k.reshape(b_lat, depth, past, A_HEADS * LANES)
    cache_v = cache_attn_v.reshape(b_lat, depth, past, A_HEADS * LANES)

    cond8 = jnp.concatenate([c_ctx[None, :], c, jnp.zeros((SUBLANES - 1 - b_lat, d), F32)], axis=0)
    mod_all = _mod_call(cond8, w_mod, b_mod)

    x = jnp.concatenate([x_prompt.reshape(n_ctx_tok, d), x_sample.reshape(b_lat * lat_seq, d)], axis=0)
    k_list, v_list, s_list = [], [], []
    for l in range(depth):
        mod_l = mod_all[l]
        lam_init = 0.8 - 0.6 * math.exp(-0.3 * l)
        proj = _inproj_call(x, mod_l, w_p[l], n_ctx_tok, lat_seq)
        k_list.append(proj[:n_ctx_tok, OFF_AK:OFF_AK + 512].reshape(b_ctx, seq, A_HEADS, 2 * A_DQK))
        v_list.append(proj[:n_ctx_tok, OFF_AV:OFF_AV + 512].reshape(b_ctx, seq, A_HEADS, A_DV))

        dl = diff_lambda[l]
        ng_a = diff_norm_g[l].reshape(1, LANES)
        oa_ctx = _attn_ctx_call(proj, dl, ng_a, b_ctx, seq, lam_init)
        oa_lat = _attn_lat_call(proj, cache_k, cache_v, l, rope_tabs, dl, ng_a, n_ctx_tok, b_lat, lat_seq,
                                lam_init)

        cvec = lambda a: a[l].reshape(1, C_CONV)
        cargs = (conv_w_p[l], cvec(conv_dw_b), cvec(conv_ln_g), cvec(conv_ln_b))
        oc_ctx = _conv_call(proj, *cargs, 0, b_ctx, seq)
        oc_lat = _conv_call(proj, *cargs, n_ctx_tok // lat_seq, b_lat, lat_seq)

        gargs = (gdn_conv_w[l], _lane_vec(gdn_A_log[l]), _lane_vec(gdn_dt_bias[l]),
                 gdn_norm_g[l].reshape(1, LANES))
        og_ctx, s_new = _gdn_call(proj, *gargs, None, 0, b_ctx, seq, True)
        og_lat, _ = _gdn_call(proj, *gargs, state_gdn[:, l], n_ctx_tok // lat_seq, b_lat, lat_seq, False)
        s_list.append(s_new)

        branches = (oa_ctx, oa_lat, oc_ctx, oc_lat, og_ctx, og_lat)
        x1, h2 = _merge_call(x, proj, branches, wa_b[l], wc_b[l], wg_b[l], wo_b[l], mod_l,
                             ln_g[l, 0:1], ln_b[l, 0:1], n_ctx_tok, lat_seq, alpha)
        x = _peer_call(h2, x1, wqt_b[l], keys_b[l], u_b[l], v_b[l], mod_l, ln_g[l, 1:2], ln_b[l, 1:2],
                       n_ctx_tok, lat_seq, alpha)

    y_prompt = x[:n_ctx_tok].reshape(b_ctx, seq, d)
    y_sample = x[n_ctx_tok:].reshape(b_lat, lat_seq, d)
    return (y_prompt, y_sample, jnp.stack(k_list, axis=1), jnp.stack(v_list, axis=1),
            jnp.stack(s_list, axis=1))
```

```python
import functools
import math

import numpy as np
import jax
import jax.numpy as jnp
from jax import lax
from jax.experimental import pallas as pl
from jax.experimental.pallas import tpu as pltpu

F32 = jnp.float32
BF16 = jnp.bfloat16

LANES = 128
SUBLANES = 8
VMEM_LIMIT = 56 * 1024 * 1024

D_MODEL = 1024
A_HEADS = 4
A_DQK = 64
A_DV = 128
GRID_W = 64
ROPE_BASE = 10000.0
C_CONV = 512
CONV_W = 31
G_HEADS = 4
G_DK = 128
G_SHORT = 3
G_CHUNK = 128
P_HEADS = 8
N_KEYS = 128
P_TOPK = 16
LN_EPS = 1e-5

N_PROJ = 8192
OFF_MG, OFF_CA, OFF_CB, OFF_AQ, OFF_AK, OFF_AV, OFF_GZ, OFF_GQ, OFF_GK, OFF_GV, OFF_GAB = (
    0, 3072, 3584, 4096, 4608, 5120, 5632, 6144, 6656, 7168, 7680)

TB_PROJ = 512
TN_PROJ = 2048
TB_MERGE = 256
TB_PEER = 512
EB_PEER = 1024


def _cparams(*sem):
    return pltpu.CompilerParams(dimension_semantics=sem, vmem_limit_bytes=VMEM_LIMIT)


def _mm(a, b):
    return jnp.dot(a.astype(BF16), b.astype(BF16), preferred_element_type=F32)


def _mm_nt(a, b):
    return lax.dot_general(a.astype(BF16), b.astype(BF16), (((1,), (1,)), ((), ())),
                           preferred_element_type=F32)


def _silu(x):
    return x * jax.nn.sigmoid(x)


def _mod_row(i, n_ctx_blocks, blocks_per_lat):
    return jnp.where(i < n_ctx_blocks, 0, 1 + (i - n_ctx_blocks) // blocks_per_lat)


def _layer_norm(x, g, b):
    mu = jnp.mean(x, axis=-1, keepdims=True)
    xc = x - mu
    var = jnp.mean(xc * xc, axis=-1, keepdims=True)
    return xc * lax.rsqrt(var + LN_EPS) * g + b


def _mod_kernel(cond_ref, w_ref, b_ref, o_ref):
    a = _silu(cond_ref[...])
    o_ref[0] = _mm(a, w_ref[0]) + b_ref[0]


def _mod_call(cond8, w_mod, b_mod):
    depth, d, n = w_mod.shape
    tn = 1536
    return pl.pallas_call(
        _mod_kernel,
        grid=(depth, n // tn),
        in_specs=[pl.BlockSpec((SUBLANES, d), lambda l, j: (0, 0)),
                  pl.BlockSpec((1, d, tn), lambda l, j: (l, 0, j)),
                  pl.BlockSpec((1, 1, tn), lambda l, j: (l, 0, j))],
        out_specs=pl.BlockSpec((1, SUBLANES, tn), lambda l, j: (l, 0, j)),
        out_shape=jax.ShapeDtypeStruct((depth, SUBLANES, n), F32),
        compiler_params=_cparams("arbitrary", "arbitrary"),
    )(cond8, w_mod, b_mod.reshape(depth, 1, n))


def _inproj_kernel(x_ref, mod_ref, w_ref, o_ref, *, n_ctx_blocks, blocks_per_lat):
    row = _mod_row(pl.program_id(1), n_ctx_blocks, blocks_per_lat)
    sh = mod_ref[pl.ds(row, 1), 0:D_MODEL]
    sc = mod_ref[pl.ds(row, 1), D_MODEL:2 * D_MODEL]
    h = x_ref[...] * (1.0 + sc) + sh
    o_ref[...] = _mm(h, w_ref[...])


def _inproj_call(x, mod_l, w_p, n_ctx_tok, lat_seq):
    n_tok = x.shape[0]
    kern = functools.partial(_inproj_kernel, n_ctx_blocks=n_ctx_tok // TB_PROJ,
                             blocks_per_lat=lat_seq // TB_PROJ)
    return pl.pallas_call(
        kern,
        grid=(N_PROJ // TN_PROJ, n_tok // TB_PROJ),
        in_specs=[pl.BlockSpec((TB_PROJ, D_MODEL), lambda j, i: (i, 0)),
                  pl.BlockSpec((SUBLANES, 6 * D_MODEL), lambda j, i: (0, 0)),
                  pl.BlockSpec((D_MODEL, TN_PROJ), lambda j, i: (0, j))],
        out_specs=pl.BlockSpec((TB_PROJ, TN_PROJ), lambda j, i: (i, j)),
        out_shape=jax.ShapeDtypeStruct((n_tok, N_PROJ), F32),
        compiler_params=_cparams("arbitrary", "arbitrary"),
    )(x, mod_l, w_p)


def _softmax(s):
    e = jnp.exp(s - jnp.max(s, axis=-1, keepdims=True))
    return e / jnp.sum(e, axis=-1, keepdims=True)


def _diff_lambda(dl_ref, lam_init):
    dl = dl_ref[...]
    a = jnp.sum(dl[0:1] * dl[1:2], axis=-1, keepdims=True)
    b = jnp.sum(dl[2:3] * dl[3:4], axis=-1, keepdims=True)
    return jnp.exp(a) - jnp.exp(b) + lam_init


def _diff_attn_heads(q, k_of, v_of, lam, ng, lam_init, o_ref):
    lo = lax.broadcasted_iota(jnp.int32, (1, LANES), 1) < A_DQK
    scale = A_DQK ** -0.5
    for h in range(A_HEADS):
        qh = q[:, h * LANES:(h + 1) * LANES]
        kh = k_of(h)
        s1 = _mm_nt(jnp.where(lo, qh, 0.0), kh) * scale
        s2 = _mm_nt(jnp.where(lo, 0.0, qh), kh) * scale
        a = _softmax(s1) - lam * _softmax(s2)
        o = _mm(a, v_of(h))
        o = o * lax.rsqrt(jnp.mean(o * o, axis=-1, keepdims=True) + LN_EPS) * ng
        o_ref[:, h * LANES:(h + 1) * LANES] = (o * (1.0 - lam_init)).astype(o_ref.dtype)


def _attn_ctx_kernel(q_ref, k_ref, v_ref, dl_ref, ng_ref, o_ref, *, lam_init):
    lam = _diff_lambda(dl_ref, lam_init)
    _diff_attn_heads(q_ref[...],
                     lambda h: k_ref[:, h * LANES:(h + 1) * LANES].astype(BF16),
                     lambda h: v_ref[:, h * LANES:(h + 1) * LANES].astype(BF16),
                     lam, ng_ref[...], lam_init, o_ref)


def _attn_ctx_call(proj, dl, ng, n_seq, seq, lam_init):
    w = A_HEADS * LANES
    spec = lambda off: pl.BlockSpec((seq, w), lambda b, off=off: (b, off // w))
    return pl.pallas_call(
        functools.partial(_attn_ctx_kernel, lam_init=lam_init),
        grid=(n_seq,),
        in_specs=[spec(OFF_AQ), spec(OFF_AK), spec(OFF_AV),
                  pl.BlockSpec(dl.shape, lambda b: (0, 0)),
                  pl.BlockSpec((1, LANES), lambda b: (0, 0))],
        out_specs=pl.BlockSpec((seq, w), lambda b: (b, 0)),
        out_shape=jax.ShapeDtypeStruct((n_seq * seq, w), BF16),
        compiler_params=_cparams("arbitrary"),
    )(proj, proj, proj, dl, ng)


def _rope(x, cos, sin_lo, sin_hi):
    n = x.shape[-1]
    return x * cos + pltpu.roll(x, n - 16, 1) * sin_lo + pltpu.roll(x, 16, 1) * sin_hi


def _attn_lat_kernel(q_ref, k_ref, v_ref, ck_ref, cv_ref, cosq_ref, slq_ref, shq_ref,
                     cosk_ref, slk_ref, shk_ref, dl_ref, ng_ref, o_ref, kall_ref, vall_ref,
                     *, lam_init, past):
    @pl.when(pl.program_id(1) == 0)
    def _():
        kall_ref[0:past, :] = ck_ref[0, 0].astype(BF16)
        vall_ref[0:past, :] = cv_ref[0, 0].astype(BF16)
        kall_ref[past:, :] = _rope(k_ref[...], cosk_ref[...], slk_ref[...], shk_ref[...]).astype(BF16)
        vall_ref[past:, :] = v_ref[...].astype(BF16)

    lam = _diff_lambda(dl_ref, lam_init)
    q = _rope(q_ref[...], cosq_ref[...], slq_ref[...], shq_ref[...])
    _diff_attn_heads(q,
                     lambda h: kall_ref[:, h * LANES:(h + 1) * LANES],
                     lambda h: vall_ref[:, h * LANES:(h + 1) * LANES],
                     lam, ng_ref[...], lam_init, o_ref)


def _attn_lat_call(proj, cache_k, cache_v, layer, rope_tabs, dl, ng, n_ctx_tok, n_seq, seq, lam_init):
    w = A_HEADS * LANES
    past = cache_k.shape[2]
    qb = 256
    nqb = seq // qb
    row0 = n_ctx_tok // seq
    rowq0 = n_ctx_tok // qb
    cos, s_lo, s_hi = rope_tabs
    qspec = pl.BlockSpec((qb, w), lambda b, i: (rowq0 + b * nqb + i, OFF_AQ // w))
    kspec = pl.BlockSpec((seq, w), lambda b, i: (row0 + b, OFF_AK // w))
    vspec = pl.BlockSpec((seq, w), lambda b, i: (row0 + b, OFF_AV // w))
    cspec = pl.BlockSpec((1, 1, past, w), lambda b, i: (b, layer, 0, 0))
    tq = pl.BlockSpec((qb, w), lambda b, i: (i, 0))
    tk = pl.BlockSpec((seq, w), lambda b, i: (0, 0))
    return pl.pallas_call(
        functools.partial(_attn_lat_kernel, lam_init=lam_init, past=past),
        grid=(n_seq, nqb),
        in_specs=[qspec, kspec, vspec, cspec, cspec, tq, tq, tq, tk, tk, tk,
                  pl.BlockSpec(dl.shape, lambda b, i: (0, 0)),
                  pl.BlockSpec((1, LANES), lambda b, i: (0, 0))],
        out_specs=pl.BlockSpec((qb, w), lambda b, i: (b * nqb + i, 0)),
        out_shape=jax.ShapeDtypeStruct((n_seq * seq, w), BF16),
        scratch_shapes=[pltpu.VMEM((past + seq, w), BF16), pltpu.VMEM((past + seq, w), BF16)],
        compiler_params=_cparams("arbitrary", "arbitrary"),
    )(proj, proj, proj, cache_k, cache_v, cos, s_lo, s_hi, cos, s_lo, s_hi, dl, ng)


CONV_PAD = 16
CONV_ROWS = 64


def _conv_kernel(ca_ref, cb_ref, w_ref, b_ref, lg_ref, lb_ref, o_ref, pad_ref, *, seq):
    zeros = jnp.zeros((CONV_PAD, C_CONV), F32)
    pad_ref[0:CONV_PAD, :] = zeros
    pad_ref[seq + CONV_PAD:seq + 2 * CONV_PAD, :] = zeros
    pad_ref[CONV_PAD:seq + CONV_PAD, :] = ca_ref[...] * jax.nn.sigmoid(cb_ref[...])
    half = CONV_W // 2

    def body(ci, carry):
        base = pl.multiple_of(ci * CONV_ROWS, CONV_ROWS)
        win = pad_ref[pl.ds(base, CONV_ROWS + 2 * CONV_PAD), :]
        acc = jnp.zeros((CONV_ROWS, C_CONV), F32)
        for r in range(SUBLANES):
            shifted = win[r:r + CONV_ROWS + 3 * SUBLANES]
            for qq in range(4):
                j = qq * SUBLANES + r - (CONV_PAD - half)
                if 0 <= j < CONV_W:
                    acc = acc + w_ref[j:j + 1, :] * shifted[qq * SUBLANES:qq * SUBLANES + CONV_ROWS]
        y = _layer_norm(acc + b_ref[...], lg_ref[...], lb_ref[...])
        o_ref[pl.ds(base, CONV_ROWS), :] = _silu(y).astype(o_ref.dtype)
        return carry

    lax.fori_loop(0, seq // CONV_ROWS, body, 0)


def _conv_call(proj, w, b, lg, lb, row0, n_seq, seq):
    spec = lambda off: pl.BlockSpec((seq, C_CONV), lambda s, off=off: (row0 + s, off // C_CONV))
    vec = pl.BlockSpec((1, C_CONV), lambda s: (0, 0))
    return pl.pallas_call(
        functools.partial(_conv_kernel, seq=seq),
        grid=(n_seq,),
        in_specs=[spec(OFF_CA), spec(OFF_CB), pl.BlockSpec(w.shape, lambda s: (0, 0)), vec, vec, vec],
        out_specs=pl.BlockSpec((seq, C_CONV), lambda s: (s, 0)),
        out_shape=jax.ShapeDtypeStruct((n_seq * seq, C_CONV), BF16),
        scratch_shapes=[pltpu.VMEM((seq + 2 * CONV_PAD, C_CONV), F32)],
        compiler_params=_cparams("arbitrary"),
    )(proj, proj, w, b, lg, lb)


GDN_GROUP = 8


def _split(a):
    hi = a.astype(BF16)
    return hi, (a - hi.astype(F32)).astype(BF16)


def _mm_split(a, b):
    (ah, al), (bh, bl) = a, b
    dot = lambda x, y: jnp.dot(x, y, preferred_element_type=F32)
    return dot(ah, bh) + (dot(ah, bl) + dot(al, bh))


def _unit_tri_inverses(l_mats):
    n = l_mats[0].shape[0]
    ri = lax.broadcasted_iota(jnp.int32, (n, n), 0)
    ci = lax.broadcasted_iota(jnp.int32, (n, n), 1)
    same_block = lambda size: (ri // size) == (ci // size)
    eye = (ri == ci).astype(F32)
    diag = [jnp.where(same_block(SUBLANES), l, 0.0) for l in l_mats]
    ps = [eye - d for d in diag]
    ms = [_split(d) for d in diag]
    for _ in range(int(math.log2(SUBLANES)) - 1):
        ms = [_split(_mm_split(m, m)) for m in ms]
        ps = [p + _mm_split(_split(p), m) for p, m in zip(ps, ms)]
    size = SUBLANES
    while size < n:
        off = same_block(2 * size) & jnp.logical_not(same_block(size))
        new_ps = []
        for p, l in zip(ps, l_mats):
            p_s = _split(p)
            t = _mm_split(_split(jnp.where(off, l, 0.0)), p_s)
            new_ps.append(p - _mm_split(p_s, _split(t)))
        ps = new_ps
        size *= 2
    return ps


def _gdn_kernel(*refs, seq, has_s0, emit_state):
    (q_ref, k_ref, v_ref, gab_ref, z_ref, cwq_ref, cwk_ref, cwv_ref, al_ref, dt_ref, ng_ref) = refs[:11]
    idx = 11
    s0_ref = None
    if has_s0:
        s0_ref = refs[idx]
        idx += 1
    o_ref = refs[idx]
    idx += 1
    sn_ref = None
    if emit_state:
        sn_ref = refs[idx]
        idx += 1
    (pad_ref, kc_s, kt_s, qc_s, vc_s, gcb_s, btb_s, grow_s, w_s, u_s, qk_s, cumt_ref, oacc_s) = refs[idx:]

    c = G_CHUNK
    n_chunks = seq // c
    head = pl.program_id(1)

    zeros8 = jnp.zeros((SUBLANES, LANES), F32)
    pad_ref[0:SUBLANES, :] = zeros8
    pad_ref[seq + SUBLANES:seq + 2 * SUBLANES, :] = zeros8

    def short_conv(x_ref, w_ref):
        pad_ref[SUBLANES:seq + SUBLANES, :] = x_ref[...]
        y = (w_ref[0:1, :] * pad_ref[SUBLANES - 1:seq + SUBLANES - 1, :]
             + w_ref[1:2, :] * pad_ref[SUBLANES:seq + SUBLANES, :]
             + w_ref[2:3, :] * pad_ref[SUBLANES + 1:seq + SUBLANES + 1, :])
        return _silu(y)

    q = short_conv(q_ref, cwq_ref)
    k = short_conv(k_ref, cwk_ref)
    v = short_conv(v_ref, cwv_ref)
    qn = q * lax.rsqrt(jnp.sum(q * q, axis=-1, keepdims=True) + 1e-6) * (G_DK ** -0.5)
    kn = k * lax.rsqrt(jnp.sum(k * k, axis=-1, keepdims=True) + 1e-6)

    gab = gab_ref[...]
    xg = gab + dt_ref[...]
    softplus = jnp.maximum(xg, 0.0) + jnp.log1p(jnp.exp(-jnp.abs(xg)))
    g_all = -jnp.exp(al_ref[...]) * softplus
    b_all = jax.nn.sigmoid(gab)

    g_t = g_all.T
    pos = lax.broadcasted_iota(jnp.int32, (1, seq), 1) % c
    yf = g_t
    yb = g_t
    s = 1
    while s < c:
        yf = yf + jnp.where(pos >= s, pltpu.roll(yf, s, 1), 0.0)
        yb = yb + jnp.where(pos < c - s, pltpu.roll(yb, seq - s, 1), 0.0)
        s *= 2
    rowi = lax.broadcasted_iota(jnp.int32, (LANES, 1), 0)
    cum_t = jnp.where(rowi < G_HEADS, yf, yb)
    cumt_ref[...] = cum_t
    cum_c = cum_t.T
    lane = lax.broadcasted_iota(jnp.int32, (1, LANES), 1)

    for d in range(2):
        r = d * G_HEADS + head
        gcol = jnp.sum(jnp.where(lane == r, cum_c, 0.0), axis=1, keepdims=True)
        bcol = jnp.sum(jnp.where(lane == 2 * G_HEADS + r, b_all, 0.0), axis=1, keepdims=True)
        grow = cumt_ref[pl.ds(r, 1), :]
        for n in range(n_chunks):
            sl = slice(n * c, (n + 1) * c)
            gcb_s[d, n] = jnp.broadcast_to(gcol[sl], (c, LANES))
            btb_s[d, n] = jnp.broadcast_to(bcol[sl], (c, LANES))
            grow_s[d, n] = jnp.broadcast_to(grow[:, sl], (SUBLANES, c))
    for n in range(n_chunks):
        sl = slice(n * c, (n + 1) * c)
        kc_s[n] = kn[sl]
        kt_s[n] = kn[sl].T
        qc_s[n] = qn[sl]
        vc_s[n] = v[sl]

    ri = lax.broadcasted_iota(jnp.int32, (c, c), 0)
    ci = lax.broadcasted_iota(jnp.int32, (c, c), 1)

    incl = [ri >= ci, ri <= ci]
    strict = [ri > ci, ri < ci]

    instances = [(d, n) for n in range(n_chunks) for d in range(2)]
    for g0 in range(0, len(instances), GDN_GROUP):
        group = instances[g0:g0 + GDN_GROUP]
        l_mats, rhs_w, rhs_u = [], [], []
        for d, n in group:
            kt = kt_s[n]
            gcb = gcb_s[d, n]
            btb = btb_s[d, n]
            decay = jnp.exp(jnp.where(incl[d], gcb - grow_s[d, n][0:1, :], -jnp.inf))
            kbeta = kc_s[n] * btb
            l_mats.append(jnp.where(strict[d], _mm(kbeta, kt) * decay, 0.0))
            rhs_w.append(_split(kbeta * jnp.exp(gcb)))
            rhs_u.append(_split(vc_s[n] * btb))
            qk_s[d, n] = jnp.where(incl[d], _mm(qc_s[n], kt) * decay, 0.0)
        invs = [_split(p) for p in _unit_tri_inverses(l_mats)]
        for (d, n), inv, rw, ru in zip(group, invs, rhs_w, rhs_u):
            w_s[d, n] = _mm_split(inv, rw)
            u_s[d, n] = _mm_split(inv, ru)

    states = [s0_ref[0, d, 0] if has_s0 else jnp.zeros((G_DK, LANES), F32) for d in range(2)]
    for i in range(n_chunks):
        for d in range(2):
            n = i if d == 0 else n_chunks - 1 - i
            gcb = gcb_s[d, n]
            glast = gcb[c - 1:c, :] if d == 0 else gcb[0:1, :]
            v_new = u_s[d, n] - _mm(w_s[d, n], states[d])
            oacc_s[d, n] = _mm(qc_s[n] * jnp.exp(gcb), states[d]) + _mm(qk_s[d, n], v_new)
            kdec_t = kt_s[n] * jnp.exp(glast - grow_s[d, n][0:1, :])
            states[d] = states[d] * jnp.exp(glast) + _mm(kdec_t, v_new)
    if emit_state:
        for d in range(2):
            sn_ref[0, d, 0] = states[d]

    ng = ng_ref[...]
    for n in range(n_chunks):
        sl = slice(n * c, (n + 1) * c)
        o = oacc_s[0, n] + oacc_s[1, n]
        o = o * lax.rsqrt(jnp.mean(o * o, axis=-1, keepdims=True) + LN_EPS) * ng
        o_ref[sl, :] = (o * _silu(z_ref[sl, :])).astype(o_ref.dtype)


def _gdn_call(proj, cw, al, dt, ng, s0, row0, n_seq, seq, emit_state):
    n_chunks = seq // G_CHUNK
    col = lambda off: pl.BlockSpec((seq, LANES), lambda b, h, off=off: (row0 + b, off // LANES + h))
    cwspec = lambda part: pl.BlockSpec((G_SHORT, LANES), lambda b, h, part=part: (0, part * G_HEADS + h))
    vec = pl.BlockSpec((1, LANES), lambda b, h: (0, 0))
    in_specs = [col(OFF_GQ), col(OFF_GK), col(OFF_GV),
                pl.BlockSpec((seq, LANES), lambda b, h: (row0 + b, OFF_GAB // LANES)),
                col(OFF_GZ), cwspec(0), cwspec(1), cwspec(2), vec, vec, vec]
    args = [proj, proj, proj, proj, proj, cw, cw, cw, al, dt, ng]
    state_spec = pl.BlockSpec((1, 2, 1, G_DK, LANES), lambda b, h: (b, 0, h, 0, 0))
    if s0 is not None:
        in_specs.append(state_spec)
        args.append(s0)
    out_specs = [pl.BlockSpec((seq, LANES), lambda b, h: (b, h))]
    out_shape = [jax.ShapeDtypeStruct((n_seq * seq, G_HEADS * LANES), BF16)]
    if emit_state:
        out_specs.append(state_spec)
        out_shape.append(jax.ShapeDtypeStruct((n_seq, 2, G_HEADS, G_DK, LANES), F32))
    chunked = lambda lead: pltpu.VMEM(lead + (G_CHUNK, LANES), F32)
    scratch = [pltpu.VMEM((seq + 2 * SUBLANES, LANES), F32),
               chunked((n_chunks,)), chunked((n_chunks,)), chunked((n_chunks,)), chunked((n_chunks,)),
               chunked((2, n_chunks)), chunked((2, n_chunks)),
               pltpu.VMEM((2, n_chunks, SUBLANES, G_CHUNK), F32),
               chunked((2, n_chunks)), chunked((2, n_chunks)), chunked((2, n_chunks)),
               pltpu.VMEM((LANES, seq), F32), chunked((2, n_chunks))]
    res = pl.pallas_call(
        functools.partial(_gdn_kernel, seq=seq, has_s0=s0 is not None, emit_state=emit_state),
        grid=(n_seq, G_HEADS),
        in_specs=in_specs,
        out_specs=out_specs,
        out_shape=out_shape,
        scratch_shapes=scratch,
        compiler_params=_cparams("arbitrary", "arbitrary"),
    )(*args)
    return res if emit_state else (res[0], None)


def _merge_kernel(x_ref, mg0_ref, mg1_ref, mg2_ref, oac_ref, oal_ref, occ_ref, ocl_ref, ogc_ref, ogl_ref,
                  wa_ref, wc_ref, wg_ref, wo_ref, mod_ref, lng_ref, lnb_ref, x1_ref, h2_ref,
                  *, n_ctx_blocks, blocks_per_lat, alpha):
    i = pl.program_id(0)
    row = _mod_row(i, n_ctx_blocks, blocks_per_lat)
    g1 = mod_ref[pl.ds(row, 1), 2 * D_MODEL:3 * D_MODEL]
    sh2 = mod_ref[pl.ds(row, 1), 3 * D_MODEL:4 * D_MODEL]
    sc2 = mod_ref[pl.ds(row, 1), 4 * D_MODEL:5 * D_MODEL]
    is_ctx = i < n_ctx_blocks
    pick = lambda c_ref, l_ref: jnp.where(is_ctx, c_ref[...], l_ref[...])
    merged = (jax.nn.sigmoid(mg0_ref[...]) * _mm(pick(oac_ref, oal_ref), wa_ref[...])
              + jax.nn.sigmoid(mg1_ref[...]) * _mm(pick(occ_ref, ocl_ref), wc_ref[...])
              + jax.nn.sigmoid(mg2_ref[...]) * _mm(pick(ogc_ref, ogl_ref), wg_ref[...]))
    mix = _mm(merged, wo_ref[...])
    x1 = _layer_norm(alpha * x_ref[...] + g1 * mix, lng_ref[...], lnb_ref[...])
    x1_ref[...] = x1
    h2_ref[...] = (x1 * (1.0 + sc2) + sh2).astype(h2_ref.dtype)


def _merge_call(x, proj, branches, wa, wc, wg, wo, mod_l, lng, lnb, n_ctx_tok, lat_seq, alpha):
    n_tok = x.shape[0]
    tb = TB_MERGE
    n_cb = n_ctx_tok // tb
    tok = lambda w: pl.BlockSpec((tb, w), lambda i: (i, 0))
    mg = lambda j: pl.BlockSpec((tb, D_MODEL), lambda i, j=j: (i, j))
    full = lambda a: pl.BlockSpec(a.shape, lambda i: (0, 0))
    ctx = pl.BlockSpec((tb, 512), lambda i: (jnp.minimum(i, n_cb - 1), 0))
    lat = pl.BlockSpec((tb, 512), lambda i: (jnp.maximum(i - n_cb, 0), 0))
    kern = functools.partial(_merge_kernel, n_ctx_blocks=n_cb, blocks_per_lat=lat_seq // tb, alpha=alpha)
    return pl.pallas_call(
        kern,
        grid=(n_tok // tb,),
        in_specs=[tok(D_MODEL), mg(0), mg(1), mg(2), ctx, lat, ctx, lat, ctx, lat,
                  full(wa), full(wc), full(wg), full(wo), full(mod_l), full(lng), full(lnb)],
        out_specs=[tok(D_MODEL), tok(D_MODEL)],
        out_shape=[jax.ShapeDtypeStruct((n_tok, D_MODEL), F32), jax.ShapeDtypeStruct((n_tok, D_MODEL), BF16)],
        compiler_params=_cparams("arbitrary"),
    )(x, proj, proj, proj, *branches, wa, wc, wg, wo, mod_l, lng, lnb)


def _top16(s):
    rowi = lax.broadcasted_iota(jnp.int32, (P_TOPK, 1), 0)
    out = jnp.zeros((P_TOPK, s.shape[1]), F32)
    w = s
    for it in range(P_TOPK):
        m = jnp.max(w, axis=0, keepdims=True)
        out = jnp.where(rowi == it, m, out)
        if it + 1 < P_TOPK:
            w = jnp.where(w >= m, -jnp.inf, w)
    return out


def _peer_select(s1, s2):
    sv1 = _top16(s1)
    sv2 = _top16(s2)
    m1 = sv1[0:1]
    m2 = sv2[0:1]
    ea = jnp.exp(sv1 - m1)
    eb = jnp.exp(sv2 - m2)
    lo, hi = slice(0, SUBLANES), slice(SUBLANES, P_TOPK)
    cands = [sv1[lo] + sv2[0:1], sv1[hi] + sv2[0:1]]
    wts = [ea[lo] * eb[0:1], ea[hi] * eb[0:1]]
    for b in range(1, SUBLANES):
        cands.append(sv1[lo] + sv2[b:b + 1])
        wts.append(ea[lo] * eb[b:b + 1])
    cands.append(sv2[hi] + sv1[0:1])
    wts.append(eb[hi] * ea[0:1])
    work = list(cands)
    tau = None
    for it in range(P_TOPK):
        tau = jnp.max(functools.reduce(jnp.maximum, work), axis=0, keepdims=True)
        if it + 1 < P_TOPK:
            work = [jnp.where(w >= tau, -jnp.inf, w) for w in work]
    z = jnp.sum(functools.reduce(lambda a, b: a + b,
                                 [jnp.where(cd >= tau, wt, 0.0) for cd, wt in zip(cands, wts)]),
                axis=0, keepdims=True)
    cnt = jnp.zeros(s1.shape, F32)
    rank2 = jnp.zeros(s2.shape, F32)
    for b in range(P_TOPK):
        cnt = jnp.where(s1 + sv2[b:b + 1] >= tau, b + 1.0, cnt)
        rank2 = jnp.where(s2 < sv2[b:b + 1], b + 1.0, rank2)
    e1 = jnp.exp(s1 - m1) * (1.0 / z)
    e2 = jnp.exp(s2 - m2)
    return cnt, rank2, e1, e2


def _peer_kernel(h2_ref, x1_ref, wqt_ref, keys_ref, u_ref, v_ref, mod_ref, lng_ref, lnb_ref, o_ref,
                 qt_ref, s_ref, e1_ref, cnt_ref, rank2_ref, e2_ref, ga_ref, acc_ref,
                 *, n_ctx_blocks, blocks_per_lat, alpha):
    j = pl.program_id(1)
    last = pl.num_programs(1) - 1
    row = _mod_row(pl.program_id(0), n_ctx_blocks, blocks_per_lat)
    tb = TB_PEER
    n_col = tb // LANES

    @pl.when(j == 0)
    def _():
        qt_ref[...] = _mm_nt(wqt_ref[...], h2_ref[...])
        for hh in range(P_HEADS):
            for p in range(2):
                r0 = (hh * 2 + p) * N_KEYS
                s_ref[hh, p] = _mm(keys_ref[p], qt_ref[r0:r0 + N_KEYS, :])
        for cc in range(n_col):
            cs = slice(cc * LANES, (cc + 1) * LANES)
            for hh in range(P_HEADS):
                cnt, rank2, e1, e2 = _peer_select(s_ref[hh, 0, :, cs], s_ref[hh, 1, :, cs])
                cnt_ref[hh, :, cs] = cnt
                e1_ref[hh, :, cs] = e1
                rank2_ref[hh, :, cs] = rank2.astype(BF16)
                e2_ref[hh, :, cs] = e2.astype(BF16)
        acc_ref[...] = jnp.zeros_like(acc_ref)

    act = _mm_nt(u_ref[...], h2_ref[...])
    gel = 0.5 * act * (1.0 + lax.erf(act * (2.0 ** -0.5)))
    i1_base = pl.multiple_of(j * SUBLANES, SUBLANES)
    for cc in range(n_col):
        cs = slice(cc * LANES, (cc + 1) * LANES)
        for half in range(EB_PEER // N_KEYS):
            g = jnp.zeros((N_KEYS, LANES), BF16)
            for hh in range(P_HEADS):
                cnt_row = cnt_ref[hh, pl.ds(i1_base, SUBLANES), cs][half:half + 1]
                e1_row = e1_ref[hh, pl.ds(i1_base, SUBLANES), cs][half:half + 1]
                cnt_b = jnp.broadcast_to(cnt_row, (N_KEYS, LANES)).astype(BF16)
                e1_b = jnp.broadcast_to(e1_row, (N_KEYS, LANES)).astype(BF16)
                sel = rank2_ref[hh, :, cs] < cnt_b
                g = g + jnp.where(sel, e1_b * e2_ref[hh, :, cs], jnp.zeros_like(g))
            ga_ref[half * N_KEYS:(half + 1) * N_KEYS, cs] = (
                g * gel[half * N_KEYS:(half + 1) * N_KEYS, cs].astype(BF16))
    acc_ref[...] += lax.dot_general(ga_ref[...], v_ref[...], (((0,), (0,)), ((), ())),
                                    preferred_element_type=F32)

    @pl.when(j == last)
    def _():
        g2 = mod_ref[pl.ds(row, 1), 5 * D_MODEL:6 * D_MODEL]
        o_ref[...] = _layer_norm(alpha * x1_ref[...] + g2 * acc_ref[...], lng_ref[...], lnb_ref[...])


def _peer_call(h2, x1, wqt, keys, u_tab, v_tab, mod_l, lng, lnb, n_ctx_tok, lat_seq, alpha):
    n_tok = h2.shape[0]
    tb = TB_PEER
    n_exp = u_tab.shape[0]
    tok = pl.BlockSpec((tb, D_MODEL), lambda i, j: (i, 0))
    full = lambda a: pl.BlockSpec(a.shape, lambda i, j: (0,) * a.ndim)
    tab = pl.BlockSpec((EB_PEER, D_MODEL), lambda i, j: (j, 0))
    kern = functools.partial(_peer_kernel, n_ctx_blocks=n_ctx_tok // tb, blocks_per_lat=lat_seq // tb,
                             alpha=alpha)
    return pl.pallas_call(
        kern,
        grid=(n_tok // tb, n_exp // EB_PEER),
        in_specs=[tok, tok, full(wqt), full(keys), tab, tab, full(mod_l), full(lng), full(lnb)],
        out_specs=tok,
        out_shape=jax.ShapeDtypeStruct((n_tok, D_MODEL), F32),
        scratch_shapes=[pltpu.VMEM((P_HEADS * 2 * N_KEYS, tb), F32),
                        pltpu.VMEM((P_HEADS, 2, N_KEYS, tb), F32),
                        pltpu.VMEM((P_HEADS, N_KEYS, tb), F32),
                        pltpu.VMEM((P_HEADS, N_KEYS, tb), F32),
                        pltpu.VMEM((P_HEADS, N_KEYS, tb), BF16),
                        pltpu.VMEM((P_HEADS, N_KEYS, tb), BF16),
                        pltpu.VMEM((EB_PEER, tb), BF16),
                        pltpu.VMEM((tb, D_MODEL), F32)],
        compiler_params=_cparams("arbitrary", "arbitrary"),
    )(h2, x1, wqt, keys, u_tab, v_tab, mod_l, lng, lnb)


def _rope_tables(n_tokens):
    n_rows = n_tokens // GRID_W
    rows = np.repeat(np.arange(n_rows, dtype=np.float32), GRID_W)
    cols = np.tile(np.arange(GRID_W, dtype=np.float32), n_rows)
    half = A_DQK // 2
    inv_freq = (1.0 / (ROPE_BASE ** (jnp.arange(0, half, 2, dtype=F32) / half)))
    ang_r = jnp.asarray(rows)[:, None] * inv_freq
    ang_c = jnp.asarray(cols)[:, None] * inv_freq
    zeros = jnp.zeros_like(ang_r)
    cr, sr, cc, sc = jnp.cos(ang_r), jnp.sin(ang_r), jnp.cos(ang_c), jnp.sin(ang_c)
    cos = jnp.concatenate([cr, cr, cc, cc], axis=-1)
    s_lo = jnp.concatenate([-sr, zeros, -sc, zeros], axis=-1)
    s_hi = jnp.concatenate([zeros, sr, zeros, sc], axis=-1)
    reps = 2 * A_HEADS
    return tuple(jnp.tile(t, (1, reps)) for t in (cos, s_lo, s_hi))


def _lane_vec(a):
    flat = a.reshape(1, -1).astype(F32)
    return jnp.pad(flat, ((0, 0), (0, LANES - flat.shape[1])))


def kernel(x_prompt, x_sample, cache_attn_k, cache_attn_v, state_gdn, c, c_ctx, w_mod, b_mod, w_in,
           diff_lambda, diff_norm_g, w_attn_o, conv_dw_w, conv_dw_b, conv_ln_g, conv_ln_b, w_conv_o,
           gdn_conv_w, gdn_A_log, gdn_dt_bias, gdn_norm_g, w_gdn_o, w_out, ln_g, ln_b,
           peer_wq, peer_keys, peer_u, peer_v):
    b_ctx, seq, d = x_prompt.shape
    b_lat, lat_seq, _ = x_sample.shape
    depth = w_mod.shape[0]
    past = cache_attn_k.shape[2]
    n_ctx_tok = b_ctx * seq
    alpha = (2 * depth) ** 0.25

    sizes = np.cumsum([0, 512, 512, 512, 1024, 1536, 512, 16, 3072])
    aq0, ak0, av0, cin0, gqkv0, gz0, gab0, mg0, end = [int(s) for s in sizes]
    w_p = jnp.concatenate([
        w_in[:, :, mg0:end], w_in[:, :, cin0:gqkv0], w_in[:, :, aq0:cin0], w_in[:, :, gz0:gab0],
        w_in[:, :, gqkv0:gz0], w_in[:, :, gab0:mg0],
        jnp.zeros((depth, d, N_PROJ - int(end)), w_in.dtype)], axis=-1).astype(BF16)
    wa_b, wc_b, wg_b, wo_b = (w.astype(BF16) for w in (w_attn_o, w_conv_o, w_gdn_o, w_out))
    wqt_b = jnp.swapaxes(peer_wq, 1, 2).astype(BF16)
    keys_b = peer_keys.astype(BF16)
    u_b = peer_u.astype(BF16)
    v_b = peer_v.astype(BF16)
    conv_w_p = jnp.pad(conv_dw_w, ((0, 0), (0, 32 - CONV_W), (0, 0)))
    rope_tabs = _rope_tables(lat_seq)
    cache_k = cache_attn_k.reshape(b_lat, depth, past, A_HEADS * LANES)
    cache_v = cache_attn_v.reshape(b_lat, depth, past, A_HEADS * LANES)

    cond8 = jnp.concatenate([c_ctx[None, :], c, jnp.zeros((SUBLANES - 1 - b_lat, d), F32)], axis=0)
    mod_all = _mod_call(cond8, w_mod, b_mod)

    x = jnp.concatenate([x_prompt.reshape(n_ctx_tok, d), x_sample.reshape(b_lat * lat_seq, d)], axis=0)
    k_list, v_list, s_list = [], [], []
    for l in range(depth):
        mod_l = mod_all[l]
        lam_init = 0.8 - 0.6 * math.exp(-0.3 * l)
        proj = _inproj_call(x, mod_l, w_p[l], n_ctx_tok, lat_seq)
        k_list.append(proj[:n_ctx_tok, OFF_AK:OFF_AK + 512].reshape(b_ctx, seq, A_HEADS, 2 * A_DQK))
        v_list.append(proj[:n_ctx_tok, OFF_AV:OFF_AV + 512].reshape(b_ctx, seq, A_HEADS, A_DV))

        dl = diff_lambda[l]
        ng_a = diff_norm_g[l].reshape(1, LANES)
        oa_ctx = _attn_ctx_call(proj, dl, ng_a, b_ctx, seq, lam_init)
        oa_lat = _attn_lat_call(proj, cache_k, cache_v, l, rope_tabs, dl, ng_a, n_ctx_tok, b_lat, lat_seq,
                                lam_init)

        cvec = lambda a: a[l].reshape(1, C_CONV)
        cargs = (conv_w_p[l], cvec(conv_dw_b), cvec(conv_ln_g), cvec(conv_ln_b))
        oc_ctx = _conv_call(proj, *cargs, 0, b_ctx, seq)
        oc_lat = _conv_call(proj, *cargs, n_ctx_tok // lat_seq, b_lat, lat_seq)

        gargs = (gdn_conv_w[l], _lane_vec(gdn_A_log[l]), _lane_vec(gdn_dt_bias[l]),
                 gdn_norm_g[l].reshape(1, LANES))
        og_ctx, s_new = _gdn_call(proj, *gargs, None, 0, b_ctx, seq, True)
        og_lat, _ = _gdn_call(proj, *gargs, state_gdn[:, l], n_ctx_tok // lat_seq, b_lat, lat_seq, False)
        s_list.append(s_new)

        branches = (oa_ctx, oa_lat, oc_ctx, oc_lat, og_ctx, og_lat)
        x1, h2 = _merge_call(x, proj, branches, wa_b[l], wc_b[l], wg_b[l], wo_b[l], mod_l,
                             ln_g[l, 0:1], ln_b[l, 0:1], n_ctx_tok, lat_seq, alpha)
        x = _peer_call(h2, x1, wqt_b[l], keys_b[l], u_b[l], v_b[l], mod_l, ln_g[l, 1:2], ln_b[l, 1:2],
                       n_ctx_tok, lat_seq, alpha)

    y_prompt = x[:n_ctx_tok].reshape(b_ctx, seq, d)
    y_sample = x[n_ctx_tok:].reshape(b_lat, lat_seq, d)
    return (y_prompt, y_sample, jnp.stack(k_list, axis=1), jnp.stack(v_list, axis=1),
            jnp.stack(s_list, axis=1))
```

```python
import functools
import math

import numpy as np
import jax
import jax.numpy as jnp
from jax import lax
from jax.experimental import pallas as pl
from jax.experimental.pallas import tpu as pltpu

F32 = jnp.float32
BF16 = jnp.bfloat16

LANES = 128
SUBLANES = 8
VMEM_LIMIT = 56 * 1024 * 1024

D_MODEL = 1024
A_HEADS = 4
A_DQK = 64
A_DV = 128
GRID_W = 64
ROPE_BASE = 10000.0
C_CONV = 512
CONV_W = 31
G_HEADS = 4
G_DK = 128
G_SHORT = 3
G_CHUNK = 128
P_HEADS = 8
N_KEYS = 128
P_TOPK = 16
LN_EPS = 1e-5

N_PROJ = 8192
OFF_MG, OFF_CA, OFF_CB, OFF_AQ, OFF_AK, OFF_AV, OFF_GZ, OFF_GQ, OFF_GK, OFF_GV, OFF_GAB = (
    0, 3072, 3584, 4096, 4608, 5120, 5632, 6144, 6656, 7168, 7680)

TB_PROJ = 512
TN_PROJ = 2048
TB_MERGE = 256
TB_PEER = 512
EB_PEER = 1024


def _cparams(*sem):
    return pltpu.CompilerParams(dimension_semantics=sem, vmem_limit_bytes=VMEM_LIMIT)


def _mm(a, b):
    return jnp.dot(a.astype(BF16), b.astype(BF16), preferred_element_type=F32)


def _mm_nt(a, b):
    return lax.dot_general(a.astype(BF16), b.astype(BF16), (((1,), (1,)), ((), ())),
                           preferred_element_type=F32)


def _silu(x):
    return x * jax.nn.sigmoid(x)


def _mod_row(i, n_ctx_blocks, blocks_per_lat):
    return jnp.where(i < n_ctx_blocks, 0, 1 + (i - n_ctx_blocks) // blocks_per_lat)


def _layer_norm(x, g, b):
    mu = jnp.mean(x, axis=-1, keepdims=True)
    xc = x - mu
    var = jnp.mean(xc * xc, axis=-1, keepdims=True)
    return xc * lax.rsqrt(var + LN_EPS) * g + b


def _mod_kernel(cond_ref, w_ref, b_ref, o_ref):
    a = _silu(cond_ref[...])
    o_ref[0] = _mm(a, w_ref[0]) + b_ref[0]


def _mod_call(cond8, w_mod, b_mod):
    depth, d, n = w_mod.shape
    tn = 1536
    return pl.pallas_call(
        _mod_kernel,
        grid=(depth, n // tn),
        in_specs=[pl.BlockSpec((SUBLANES, d), lambda l, j: (0, 0)),
                  pl.BlockSpec((1, d, tn), lambda l, j: (l, 0, j)),
                  pl.BlockSpec((1, 1, tn), lambda l, j: (l, 0, j))],
        out_specs=pl.BlockSpec((1, SUBLANES, tn), lambda l, j: (l, 0, j)),
        out_shape=jax.ShapeDtypeStruct((depth, SUBLANES, n), F32),
        compiler_params=_cparams("arbitrary", "arbitrary"),
    )(cond8, w_mod, b_mod.reshape(depth, 1, n))


def _inproj_kernel(x_ref, mod_ref, w_ref, o_ref, *, n_ctx_blocks, blocks_per_lat):
    row = _mod_row(pl.program_id(1), n_ctx_blocks, blocks_per_lat)
    sh = mod_ref[pl.ds(row, 1), 0:D_MODEL]
    sc = mod_ref[pl.ds(row, 1), D_MODEL:2 * D_MODEL]
    h = x_ref[...] * (1.0 + sc) + sh
    o_ref[...] = _mm(h, w_ref[...])


def _inproj_call(x, mod_l, w_p, n_ctx_tok, lat_seq):
    n_tok = x.shape[0]
    kern = functools.partial(_inproj_kernel, n_ctx_blocks=n_ctx_tok // TB_PROJ,
                             blocks_per_lat=lat_seq // TB_PROJ)
    return pl.pallas_call(
        kern,
        grid=(N_PROJ // TN_PROJ, n_tok // TB_PROJ),
        in_specs=[pl.BlockSpec((TB_PROJ, D_MODEL), lambda j, i: (i, 0)),
                  pl.BlockSpec((SUBLANES, 6 * D_MODEL), lambda j, i: (0, 0)),
                  pl.BlockSpec((D_MODEL, TN_PROJ), lambda j, i: (0, j))],
        out_specs=pl.BlockSpec((TB_PROJ, TN_PROJ), lambda j, i: (i, j)),
        out_shape=jax.ShapeDtypeStruct((n_tok, N_PROJ), F32),
        compiler_params=_cparams("arbitrary", "arbitrary"),
    )(x, mod_l, w_p)


def _softmax(s):
    e = jnp.exp(s - jnp.max(s, axis=-1, keepdims=True))
    return e / jnp.sum(e, axis=-1, keepdims=True)


def _diff_lambda(dl_ref, lam_init):
    dl = dl_ref[...]
    a = jnp.sum(dl[0:1] * dl[1:2], axis=-1, keepdims=True)
    b = jnp.sum(dl[2:3] * dl[3:4], axis=-1, keepdims=True)
    return jnp.exp(a) - jnp.exp(b) + lam_init


def _diff_attn_heads(q, k_of, v_of, lam, ng, lam_init, o_ref):
    lo = lax.broadcasted_iota(jnp.int32, (1, LANES), 1) < A_DQK
    scale = A_DQK ** -0.5
    for h in range(A_HEADS):
        qh = q[:, h * LANES:(h + 1) * LANES]
        kh = k_of(h)
        s1 = _mm_nt(jnp.where(lo, qh, 0.0), kh) * scale
        s2 = _mm_nt(jnp.where(lo, 0.0, qh), kh) * scale
        a = _softmax(s1) - lam * _softmax(s2)
        o = _mm(a, v_of(h))
        o = o * lax.rsqrt(jnp.mean(o * o, axis=-1, keepdims=True) + LN_EPS) * ng
        o_ref[:, h * LANES:(h + 1) * LANES] = (o * (1.0 - lam_init)).astype(o_ref.dtype)


def _attn_ctx_kernel(q_ref, k_ref, v_ref, dl_ref, ng_ref, o_ref, *, lam_init):
    lam = _diff_lambda(dl_ref, lam_init)
    _diff_attn_heads(q_ref[...],
                     lambda h: k_ref[:, h * LANES:(h + 1) * LANES].astype(BF16),
                     lambda h: v_ref[:, h * LANES:(h + 1) * LANES].astype(BF16),
                     lam, ng_ref[...], lam_init, o_ref)


def _attn_ctx_call(proj, dl, ng, n_seq, seq, lam_init):
    w = A_HEADS * LANES
    spec = lambda off: pl.BlockSpec((seq, w), lambda b, off=off: (b, off // w))
    return pl.pallas_call(
        functools.partial(_attn_ctx_kernel, lam_init=lam_init),
        grid=(n_seq,),
        in_specs=[spec(OFF_AQ), spec(OFF_AK), spec(OFF_AV),
                  pl.BlockSpec(dl.shape, lambda b: (0, 0)),
                  pl.BlockSpec((1, LANES), lambda b: (0, 0))],
        out_specs=pl.BlockSpec((seq, w), lambda b: (b, 0)),
        out_shape=jax.ShapeDtypeStruct((n_seq * seq, w), BF16),
        compiler_params=_cparams("arbitrary"),
    )(proj, proj, proj, dl, ng)


def _rope(x, cos, sin_lo, sin_hi):
    n = x.shape[-1]
    return x * cos + pltpu.roll(x, n - 16, 1) * sin_lo + pltpu.roll(x, 16, 1) * sin_hi


def _attn_lat_kernel(q_ref, k_ref, v_ref, ck_ref, cv_ref, cosq_ref, slq_ref, shq_ref,
                     cosk_ref, slk_ref, shk_ref, dl_ref, ng_ref, o_ref, kall_ref, vall_ref,
                     *, lam_init, past):
    @pl.when(pl.program_id(1) == 0)
    def _():
        kall_ref[0:past, :] = ck_ref[0, 0].astype(BF16)
        vall_ref[0:past, :] = cv_ref[0, 0].astype(BF16)
        kall_ref[past:, :] = _rope(k_ref[...], cosk_ref[...], slk_ref[...], shk_ref[...]).astype(BF16)
        vall_ref[past:, :] = v_ref[...].astype(BF16)

    lam = _diff_lambda(dl_ref, lam_init)
    q = _rope(q_ref[...], cosq_ref[...], slq_ref[...], shq_ref[...])
    _diff_attn_heads(q,
                     lambda h: kall_ref[:, h * LANES:(h + 1) * LANES],
                     lambda h: vall_ref[:, h * LANES:(h + 1) * LANES],
                     lam, ng_ref[...], lam_init, o_ref)


def _attn_lat_call(proj, cache_k, cache_v, layer, rope_tabs, dl, ng, n_ctx_tok, n_seq, seq, lam_init):
    w = A_HEADS * LANES
    past = cache_k.shape[2]
    qb = 256
    nqb = seq // qb
    row0 = n_ctx_tok // seq
    rowq0 = n_ctx_tok // qb
    cos, s_lo, s_hi = rope_tabs
    qspec = pl.BlockSpec((qb, w), lambda b, i: (rowq0 + b * nqb + i, OFF_AQ // w))
    kspec = pl.BlockSpec((seq, w), lambda b, i: (row0 + b, OFF_AK // w))
    vspec = pl.BlockSpec((seq, w), lambda b, i: (row0 + b, OFF_AV // w))
    cspec = pl.BlockSpec((1, 1, past, w), lambda b, i: (b, layer, 0, 0))
    tq = pl.BlockSpec((qb, w), lambda b, i: (i, 0))
    tk = pl.BlockSpec((seq, w), lambda b, i: (0, 0))
    return pl.pallas_call(
        functools.partial(_attn_lat_kernel, lam_init=lam_init, past=past),
        grid=(n_seq, nqb),
        in_specs=[qspec, kspec, vspec, cspec, cspec, tq, tq, tq, tk, tk, tk,
                  pl.BlockSpec(dl.shape, lambda b, i: (0, 0)),
                  pl.BlockSpec((1, LANES), lambda b, i: (0, 0))],
        out_specs=pl.BlockSpec((qb, w), lambda b, i: (b * nqb + i, 0)),
        out_shape=jax.ShapeDtypeStruct((n_seq * seq, w), BF16),
        scratch_shapes=[pltpu.VMEM((past + seq, w), BF16), pltpu.VMEM((past + seq, w), BF16)],
        compiler_params=_cparams("arbitrary", "arbitrary"),
    )(proj, proj, proj, cache_k, cache_v, cos, s_lo, s_hi, cos, s_lo, s_hi, dl, ng)


CONV_PAD = 16
CONV_ROWS = 64


def _conv_kernel(ca_ref, cb_ref, w_ref, b_ref, lg_ref, lb_ref, o_ref, pad_ref, *, seq):
    zeros = jnp.zeros((CONV_PAD, C_CONV), F32)
    pad_ref[0:CONV_PAD, :] = zeros
    pad_ref[seq + CONV_PAD:seq + 2 * CONV_PAD, :] = zeros
    pad_ref[CONV_PAD:seq + CONV_PAD, :] = ca_ref[...] * jax.nn.sigmoid(cb_ref[...])
    half = CONV_W // 2

    def body(ci, carry):
        base = pl.multiple_of(ci * CONV_ROWS, CONV_ROWS)
        win = pad_ref[pl.ds(base, CONV_ROWS + 2 * CONV_PAD), :]
        acc = jnp.zeros((CONV_ROWS, C_CONV), F32)
        for r in range(SUBLANES):
            shifted = win[r:r + CONV_ROWS + 3 * SUBLANES]
            for qq in range(4):
                j = qq * SUBLANES + r - (CONV_PAD - half)
                if 0 <= j < CONV_W:
                    acc = acc + w_ref[j:j + 1, :] * shifted[qq * SUBLANES:qq * SUBLANES + CONV_ROWS]
        y = _layer_norm(acc + b_ref[...], lg_ref[...], lb_ref[...])
        o_ref[pl.ds(base, CONV_ROWS), :] = _silu(y).astype(o_ref.dtype)
        return carry

    lax.fori_loop(0, seq // CONV_ROWS, body, 0)


def _conv_call(proj, w, b, lg, lb, row0, n_seq, seq):
    spec = lambda off: pl.BlockSpec((seq, C_CONV), lambda s, off=off: (row0 + s, off // C_CONV))
    vec = pl.BlockSpec((1, C_CONV), lambda s: (0, 0))
    return pl.pallas_call(
        functools.partial(_conv_kernel, seq=seq),
        grid=(n_seq,),
        in_specs=[spec(OFF_CA), spec(OFF_CB), pl.BlockSpec(w.shape, lambda s: (0, 0)), vec, vec, vec],
        out_specs=pl.BlockSpec((seq, C_CONV), lambda s: (s, 0)),
        out_shape=jax.ShapeDtypeStruct((n_seq * seq, C_CONV), BF16),
        scratch_shapes=[pltpu.VMEM((seq + 2 * CONV_PAD, C_CONV), F32)],
        compiler_params=_cparams("arbitrary"),
    )(proj, proj, w, b, lg, lb)


GDN_GROUP = 8
GDN_HEADS_CTX = 4
GDN_HEADS_LAT = 2


def _split(a):
    hi = a.astype(BF16)
    return hi, (a - hi.astype(F32)).astype(BF16)


def _mm_split(a, b):
    (ah, al), (bh, bl) = a, b
    n = bh.shape[1]
    lhs = jnp.concatenate([ah, al], axis=1)
    rhs = jnp.concatenate([jnp.concatenate([bh, bl], axis=1),
                           jnp.concatenate([bh, jnp.zeros_like(bl)], axis=1)], axis=0)
    out = jnp.dot(lhs, rhs, preferred_element_type=F32)
    return out[:, :n] + out[:, n:]


def _unit_tri_inverses(l_mats):
    n = l_mats[0].shape[0]
    ri = lax.broadcasted_iota(jnp.int32, (n, n), 0)
    ci = lax.broadcasted_iota(jnp.int32, (n, n), 1)
    same_block = lambda size: (ri // size) == (ci // size)
    eye = (ri == ci).astype(F32)
    diag = [jnp.where(same_block(SUBLANES), l, 0.0) for l in l_mats]
    ps = [eye - d for d in diag]
    ms = [_split(d) for d in diag]
    for _ in range(int(math.log2(SUBLANES)) - 1):
        ms = [_split(_mm_split(m, m)) for m in ms]
        ps = [p + _mm_split(_split(p), m) for p, m in zip(ps, ms)]
    size = SUBLANES
    while size < n:
        off = same_block(2 * size) & jnp.logical_not(same_block(size))
        new_ps = []
        for p, l in zip(ps, l_mats):
            p_s = _split(p)
            t = _mm_split(_split(jnp.where(off, l, 0.0)), p_s)
            new_ps.append(p - _mm_split(p_s, _split(t)))
        ps = new_ps
        size *= 2
    return ps


def _gdn_kernel(*refs, seq, heads, has_s0, emit_state):
    (q_ref, k_ref, v_ref, gab_ref, z_ref, cwq_ref, cwk_ref, cwv_ref, al_ref, dt_ref, ng_ref) = refs[:11]
    idx = 11
    s0_ref = None
    if has_s0:
        s0_ref = refs[idx]
        idx += 1
    o_ref = refs[idx]
    idx += 1
    sn_ref = None
    if emit_state:
        sn_ref = refs[idx]
        idx += 1
    (pad_ref, kc_s, kt_s, qc_s, vc_s, gcb_s, btb_s, grow_s, w_s, u_s, qk_s, cumt_ref, oacc_s) = refs[idx:]

    c = G_CHUNK
    n_chunks = seq // c
    head0 = pl.program_id(1) * heads
    hsl = lambda hl: slice(hl * LANES, (hl + 1) * LANES)

    zeros8 = jnp.zeros((SUBLANES, heads * LANES), F32)
    pad_ref[0:SUBLANES, :] = zeros8
    pad_ref[seq + SUBLANES:seq + 2 * SUBLANES, :] = zeros8

    def short_conv(x_ref, w_ref):
        pad_ref[SUBLANES:seq + SUBLANES, :] = x_ref[...]
        y = (w_ref[0:1, :] * pad_ref[SUBLANES - 1:seq + SUBLANES - 1, :]
             + w_ref[1:2, :] * pad_ref[SUBLANES:seq + SUBLANES, :]
             + w_ref[2:3, :] * pad_ref[SUBLANES + 1:seq + SUBLANES + 1, :])
        return _silu(y)

    q = short_conv(q_ref, cwq_ref)
    k = short_conv(k_ref, cwk_ref)
    v = short_conv(v_ref, cwv_ref)

    gab = gab_ref[...]
    xg = gab + dt_ref[...]
    softplus = jnp.maximum(xg, 0.0) + jnp.log1p(jnp.exp(-jnp.abs(xg)))
    g_all = -jnp.exp(al_ref[...]) * softplus
    b_all = jax.nn.sigmoid(gab)

    g_t = g_all.T
    pos = lax.broadcasted_iota(jnp.int32, (1, seq), 1) % c
    yf = g_t
    yb = g_t
    s = 1
    while s < c:
        yf = yf + jnp.where(pos >= s, pltpu.roll(yf, s, 1), 0.0)
        yb = yb + jnp.where(pos < c - s, pltpu.roll(yb, seq - s, 1), 0.0)
        s *= 2
    rowi = lax.broadcasted_iota(jnp.int32, (LANES, 1), 0)
    cum_t = jnp.where(rowi < G_HEADS, yf, yb)
    cumt_ref[...] = cum_t
    cum_c = cum_t.T
    lane = lax.broadcasted_iota(jnp.int32, (1, LANES), 1)

    for hl in range(heads):
        qh = q[:, hsl(hl)]
        kh = k[:, hsl(hl)]
        qn = qh * lax.rsqrt(jnp.sum(qh * qh, axis=-1, keepdims=True) + 1e-6) * (G_DK ** -0.5)
        kn = kh * lax.rsqrt(jnp.sum(kh * kh, axis=-1, keepdims=True) + 1e-6)
        for n in range(n_chunks):
            sl = slice(n * c, (n + 1) * c)
            kc_s[hl, n] = kn[sl]
            kt_s[hl, n] = kn[sl].T
            qc_s[hl, n] = qn[sl]
            vc_s[hl, n] = v[sl, hsl(hl)]
        for d in range(2):
            r = d * G_HEADS + head0 + hl
            gcol = jnp.sum(jnp.where(lane == r, cum_c, 0.0), axis=1, keepdims=True)
            bcol = jnp.sum(jnp.where(lane == 2 * G_HEADS + r, b_all, 0.0), axis=1, keepdims=True)
            grow = cumt_ref[pl.ds(r, 1), :]
            for n in range(n_chunks):
                sl = slice(n * c, (n + 1) * c)
                gcb_s[hl, d, n] = jnp.broadcast_to(gcol[sl], (c, LANES))
                btb_s[hl, d, n] = jnp.broadcast_to(bcol[sl], (c, LANES))
                grow_s[hl, d, n] = jnp.broadcast_to(grow[:, sl], (SUBLANES, c))

    ri = lax.broadcasted_iota(jnp.int32, (c, c), 0)
    ci = lax.broadcasted_iota(jnp.int32, (c, c), 1)
    incl = [ri >= ci, ri <= ci]
    strict = [ri > ci, ri < ci]

    instances = [(hl, d, n) for n in range(n_chunks) for d in range(2) for hl in range(heads)]
    for g0 in range(0, len(instances), GDN_GROUP):
        group = instances[g0:g0 + GDN_GROUP]
        l_mats, rhs_w, rhs_u = [], [], []
        for hl, d, n in group:
            kt = kt_s[hl, n]
            gcb = gcb_s[hl, d, n]
            btb = btb_s[hl, d, n]
            decay = jnp.exp(jnp.where(incl[d], gcb - grow_s[hl, d, n][0:1, :], -jnp.inf))
            kbeta = kc_s[hl, n] * btb
            l_mats.append(jnp.where(strict[d], _mm(kbeta, kt) * decay, 0.0))
            rhs_w.append(_split(kbeta * jnp.exp(gcb)))
            rhs_u.append(_split(vc_s[hl, n] * btb))
            qk_s[hl, d, n] = jnp.where(incl[d], _mm(qc_s[hl, n], kt) * decay, 0.0)
        invs = [_split(p) for p in _unit_tri_inverses(l_mats)]
        for (hl, d, n), inv, rw, ru in zip(group, invs, rhs_w, rhs_u):
            w_s[hl, d, n] = _mm_split(inv, rw)
            u_s[hl, d, n] = _mm_split(inv, ru)

    states = [[s0_ref[0, d, hl] if has_s0 else jnp.zeros((G_DK, LANES), F32) for d in range(2)]
              for hl in range(heads)]
    for i in range(n_chunks):
        for hl in range(heads):
            for d in range(2):
                n = i if d == 0 else n_chunks - 1 - i
                gcb = gcb_s[hl, d, n]
                glast = gcb[c - 1:c, :] if d == 0 else gcb[0:1, :]
                state = states[hl][d]
                v_new = u_s[hl, d, n] - _mm(w_s[hl, d, n], state)
                oacc_s[hl, d, n] = _mm(qc_s[hl, n] * jnp.exp(gcb), state) + _mm(qk_s[hl, d, n], v_new)
                kdec_t = kt_s[hl, n] * jnp.exp(glast - grow_s[hl, d, n][0:1, :])
                states[hl][d] = state * jnp.exp(glast) + _mm(kdec_t, v_new)
    if emit_state:
        for hl in range(heads):
            for d in range(2):
                sn_ref[0, d, hl] = states[hl][d]

    ng = ng_ref[...]
    for hl in range(heads):
        for n in range(n_chunks):
            sl = slice(n * c, (n + 1) * c)
            o = oacc_s[hl, 0, n] + oacc_s[hl, 1, n]
            o = o * lax.rsqrt(jnp.mean(o * o, axis=-1, keepdims=True) + LN_EPS) * ng
            o_ref[sl, hsl(hl)] = (o * _silu(z_ref[sl, hsl(hl)])).astype(o_ref.dtype)


def _gdn_call(proj, cw, al, dt, ng, s0, row0, n_seq, seq, heads, emit_state):
    n_chunks = seq // G_CHUNK
    w = heads * LANES
    n_hb = G_HEADS // heads
    col = lambda off: pl.BlockSpec((seq, w), lambda b, h, off=off: (row0 + b, off // w + h))
    cwspec = lambda part: pl.BlockSpec((G_SHORT, w), lambda b, h, part=part: (0, part * n_hb + h))
    vec = pl.BlockSpec((1, LANES), lambda b, h: (0, 0))
    in_specs = [col(OFF_GQ), col(OFF_GK), col(OFF_GV),
                pl.BlockSpec((seq, LANES), lambda b, h: (row0 + b, OFF_GAB // LANES)),
                col(OFF_GZ), cwspec(0), cwspec(1), cwspec(2), vec, vec, vec]
    args = [proj, proj, proj, proj, proj, cw, cw, cw, al, dt, ng]
    state_spec = pl.BlockSpec((1, 2, heads, G_DK, LANES), lambda b, h: (b, 0, h, 0, 0))
    if s0 is not None:
        in_specs.append(state_spec)
        args.append(s0)
    out_specs = [pl.BlockSpec((seq, w), lambda b, h: (b, h))]
    out_shape = [jax.ShapeDtypeStruct((n_seq * seq, G_HEADS * LANES), BF16)]
    if emit_state:
        out_specs.append(state_spec)
        out_shape.append(jax.ShapeDtypeStruct((n_seq, 2, G_HEADS, G_DK, LANES), F32))
    chunked = lambda lead: pltpu.VMEM(lead + (G_CHUNK, LANES), F32)
    scratch = [pltpu.VMEM((seq + 2 * SUBLANES, w), F32),
               chunked((heads, n_chunks)), chunked((heads, n_chunks)), chunked((heads, n_chunks)),
               chunked((heads, n_chunks)),
               chunked((heads, 2, n_chunks)), chunked((heads, 2, n_chunks)),
               pltpu.VMEM((heads, 2, n_chunks, SUBLANES, G_CHUNK), F32),
               chunked((heads, 2, n_chunks)), chunked((heads, 2, n_chunks)), chunked((heads, 2, n_chunks)),
               pltpu.VMEM((LANES, seq), F32), chunked((heads, 2, n_chunks))]
    res = pl.pallas_call(
        functools.partial(_gdn_kernel, seq=seq, heads=heads, has_s0=s0 is not None, emit_state=emit_state),
        grid=(n_seq, n_hb),
        in_specs=in_specs,
        out_specs=out_specs,
        out_shape=out_shape,
        scratch_shapes=scratch,
        compiler_params=_cparams("arbitrary", "arbitrary"),
    )(*args)
    return res if emit_state else (res[0], None)


def _merge_kernel(x_ref, mg0_ref, mg1_ref, mg2_ref, oac_ref, oal_ref, occ_ref, ocl_ref, ogc_ref, ogl_ref,
                  wa_ref, wc_ref, wg_ref, wo_ref, mod_ref, lng_ref, lnb_ref, x1_ref, h2_ref,
                  *, n_ctx_blocks, blocks_per_lat, alpha):
    i = pl.program_id(0)
    row = _mod_row(i, n_ctx_blocks, blocks_per_lat)
    g1 = mod_ref[pl.ds(row, 1), 2 * D_MODEL:3 * D_MODEL]
    sh2 = mod_ref[pl.ds(row, 1), 3 * D_MODEL:4 * D_MODEL]
    sc2 = mod_ref[pl.ds(row, 1), 4 * D_MODEL:5 * D_MODEL]
    is_ctx = i < n_ctx_blocks
    pick = lambda c_ref, l_ref: jnp.where(is_ctx, c_ref[...], l_ref[...])
    merged = (jax.nn.sigmoid(mg0_ref[...]) * _mm(pick(oac_ref, oal_ref), wa_ref[...])
              + jax.nn.sigmoid(mg1_ref[...]) * _mm(pick(occ_ref, ocl_ref), wc_ref[...])
              + jax.nn.sigmoid(mg2_ref[...]) * _mm(pick(ogc_ref, ogl_ref), wg_ref[...]))
    mix = _mm(merged, wo_ref[...])
    x1 = _layer_norm(alpha * x_ref[...] + g1 * mix, lng_ref[...], lnb_ref[...])
    x1_ref[...] = x1
    h2_ref[...] = (x1 * (1.0 + sc2) + sh2).astype(h2_ref.dtype)


def _merge_call(x, proj, branches, wa, wc, wg, wo, mod_l, lng, lnb, n_ctx_tok, lat_seq, alpha):
    n_tok = x.shape[0]
    tb = TB_MERGE
    n_cb = n_ctx_tok // tb
    tok = lambda w: pl.BlockSpec((tb, w), lambda i: (i, 0))
    mg = lambda j: pl.BlockSpec((tb, D_MODEL), lambda i, j=j: (i, j))
    full = lambda a: pl.BlockSpec(a.shape, lambda i: (0, 0))
    ctx = pl.BlockSpec((tb, 512), lambda i: (jnp.minimum(i, n_cb - 1), 0))
    lat = pl.BlockSpec((tb, 512), lambda i: (jnp.maximum(i - n_cb, 0), 0))
    kern = functools.partial(_merge_kernel, n_ctx_blocks=n_cb, blocks_per_lat=lat_seq // tb, alpha=alpha)
    return pl.pallas_call(
        kern,
        grid=(n_tok // tb,),
        in_specs=[tok(D_MODEL), mg(0), mg(1), mg(2), ctx, lat, ctx, lat, ctx, lat,
                  full(wa), full(wc), full(wg), full(wo), full(mod_l), full(lng), full(lnb)],
        out_specs=[tok(D_MODEL), tok(D_MODEL)],
        out_shape=[jax.ShapeDtypeStruct((n_tok, D_MODEL), F32), jax.ShapeDtypeStruct((n_tok, D_MODEL), BF16)],
        compiler_params=_cparams("arbitrary"),
    )(x, proj, proj, proj, *branches, wa, wc, wg, wo, mod_l, lng, lnb)


def _top16(s):
    rowi = lax.broadcasted_iota(jnp.int32, (P_TOPK, 1), 0)
    out = jnp.zeros((P_TOPK, s.shape[1]), F32)
    w = s
    for it in range(P_TOPK):
        m = jnp.max(w, axis=0, keepdims=True)
        out = jnp.where(rowi == it, m, out)
        if it + 1 < P_TOPK:
            w = jnp.where(w >= m, -jnp.inf, w)
    return out


def _peer_select(s1, s2):
    sv1 = _top16(s1)
    sv2 = _top16(s2)
    m1 = sv1[0:1]
    m2 = sv2[0:1]
    ea = jnp.exp(sv1 - m1)
    eb = jnp.exp(sv2 - m2)
    lo, hi = slice(0, SUBLANES), slice(SUBLANES, P_TOPK)
    cands = [sv1[lo] + sv2[0:1], sv1[hi] + sv2[0:1]]
    wts = [ea[lo] * eb[0:1], ea[hi] * eb[0:1]]
    for b in range(1, SUBLANES):
        cands.append(sv1[lo] + sv2[b:b + 1])
        wts.append(ea[lo] * eb[b:b + 1])
    cands.append(sv2[hi] + sv1[0:1])
    wts.append(eb[hi] * ea[0:1])
    work = list(cands)
    tau = None
    for it in range(P_TOPK):
        tau = jnp.max(functools.reduce(jnp.maximum, work), axis=0, keepdims=True)
        if it + 1 < P_TOPK:
            work = [jnp.where(w >= tau, -jnp.inf, w) for w in work]
    z = jnp.sum(functools.reduce(lambda a, b: a + b,
                                 [jnp.where(cd >= tau, wt, 0.0) for cd, wt in zip(cands, wts)]),
                axis=0, keepdims=True)
    thr = jnp.full(s1.shape, jnp.inf, F32)
    for b in range(P_TOPK):
        thr = jnp.where(s1 + sv2[b:b + 1] >= tau, sv2[b:b + 1], thr)
    e1 = jnp.exp(s1 - m1) * (1.0 / z)
    e2 = jnp.exp(s2 - m2)
    return thr, e1, e2


def _peer_kernel(h2_ref, x1_ref, wqt_ref, keys_ref, u_ref, v_ref, mod_ref, lng_ref, lnb_ref, o_ref,
                 qt_ref, s_ref, e_ref, thr_ref, ga_ref, acc_ref, *, n_ctx_blocks, blocks_per_lat, alpha):
    j = pl.program_id(1)
    last = pl.num_programs(1) - 1
    row = _mod_row(pl.program_id(0), n_ctx_blocks, blocks_per_lat)
    tb = TB_PEER
    n_col = tb // LANES

    @pl.when(j == 0)
    def _():
        qt_ref[...] = _mm_nt(wqt_ref[...], h2_ref[...])
        for hh in range(P_HEADS):
            for p in range(2):
                r0 = (hh * 2 + p) * N_KEYS
                s_ref[hh, p] = _mm(keys_ref[p], qt_ref[r0:r0 + N_KEYS, :])
        for cc in range(n_col):
            cs = slice(cc * LANES, (cc + 1) * LANES)
            for hh in range(P_HEADS):
                thr, e1, e2 = _peer_select(s_ref[hh, 0, :, cs], s_ref[hh, 1, :, cs])
                thr_ref[hh, :, cs] = thr
                e_ref[hh, 0, :, cs] = e1
                e_ref[hh, 1, :, cs] = e2
        acc_ref[...] = jnp.zeros_like(acc_ref)

    act = _mm_nt(u_ref[...], h2_ref[...])
    gel = 0.5 * act * (1.0 + lax.erf(act * (2.0 ** -0.5)))
    i1_base = pl.multiple_of(j * SUBLANES, SUBLANES)
    for cc in range(n_col):
        cs = slice(cc * LANES, (cc + 1) * LANES)
        for half in range(EB_PEER // N_KEYS):
            g = jnp.zeros((N_KEYS, LANES), F32)
            for hh in range(P_HEADS):
                thr_row = thr_ref[hh, pl.ds(i1_base, SUBLANES), cs][half:half + 1]
                e1row = e_ref[hh, 0, pl.ds(i1_base, SUBLANES), cs][half:half + 1]
                sel = s_ref[hh, 1, :, cs] >= thr_row
                g = g + jnp.where(sel, e1row * e_ref[hh, 1, :, cs], 0.0)
            ga_ref[half * N_KEYS:(half + 1) * N_KEYS, cs] = (
                g * gel[half * N_KEYS:(half + 1) * N_KEYS, cs]).astype(BF16)
    acc_ref[...] += lax.dot_general(ga_ref[...], v_ref[...], (((0,), (0,)), ((), ())),
                                    preferred_element_type=F32)

    @pl.when(j == last)
    def _():
        g2 = mod_ref[pl.ds(row, 1), 5 * D_MODEL:6 * D_MODEL]
        o_ref[...] = _layer_norm(alpha * x1_ref[...] + g2 * acc_ref[...], lng_ref[...], lnb_ref[...])


def _peer_call(h2, x1, wqt, keys, u_tab, v_tab, mod_l, lng, lnb, n_ctx_tok, lat_seq, alpha):
    n_tok = h2.shape[0]
    tb = TB_PEER
    n_exp = u_tab.shape[0]
    tok = pl.BlockSpec((tb, D_MODEL), lambda i, j: (i, 0))
    full = lambda a: pl.BlockSpec(a.shape, lambda i, j: (0,) * a.ndim)
    tab = pl.BlockSpec((EB_PEER, D_MODEL), lambda i, j: (j, 0))
    kern = functools.partial(_peer_kernel, n_ctx_blocks=n_ctx_tok // tb, blocks_per_lat=lat_seq // tb,
                             alpha=alpha)
    return pl.pallas_call(
        kern,
        grid=(n_tok // tb, n_exp // EB_PEER),
        in_specs=[tok, tok, full(wqt), full(keys), tab, tab, full(mod_l), full(lng), full(lnb)],
        out_specs=tok,
        out_shape=jax.ShapeDtypeStruct((n_tok, D_MODEL), F32),
        scratch_shapes=[pltpu.VMEM((P_HEADS * 2 * N_KEYS, tb), F32),
                        pltpu.VMEM((P_HEADS, 2, N_KEYS, tb), F32),
                        pltpu.VMEM((P_HEADS, 2, N_KEYS, tb), F32),
                        pltpu.VMEM((P_HEADS, N_KEYS, tb), F32),
                        pltpu.VMEM((EB_PEER, tb), BF16),
                        pltpu.VMEM((tb, D_MODEL), F32)],
        compiler_params=_cparams("arbitrary", "arbitrary"),
    )(h2, x1, wqt, keys, u_tab, v_tab, mod_l, lng, lnb)


def _rope_tables(n_tokens):
    n_rows = n_tokens // GRID_W
    rows = np.repeat(np.arange(n_rows, dtype=np.float32), GRID_W)
    cols = np.tile(np.arange(GRID_W, dtype=np.float32), n_rows)
    half = A_DQK // 2
    inv_freq = (1.0 / (ROPE_BASE ** (jnp.arange(0, half, 2, dtype=F32) / half)))
    ang_r = jnp.asarray(rows)[:, None] * inv_freq
    ang_c = jnp.asarray(cols)[:, None] * inv_freq
    zeros = jnp.zeros_like(ang_r)
    cr, sr, cc, sc = jnp.cos(ang_r), jnp.sin(ang_r), jnp.cos(ang_c), jnp.sin(ang_c)
    cos = jnp.concatenate([cr, cr, cc, cc], axis=-1)
    s_lo = jnp.concatenate([-sr, zeros, -sc, zeros], axis=-1)
    s_hi = jnp.concatenate([zeros, sr, zeros, sc], axis=-1)
    reps = 2 * A_HEADS
    return tuple(jnp.tile(t, (1, reps)) for t in (cos, s_lo, s_hi))


def _lane_vec(a):
    flat = a.reshape(1, -1).astype(F32)
    return jnp.pad(flat, ((0, 0), (0, LANES - flat.shape[1])))


def kernel(x_prompt, x_sample, cache_attn_k, cache_attn_v, state_gdn, c, c_ctx, w_mod, b_mod, w_in,
           diff_lambda, diff_norm_g, w_attn_o, conv_dw_w, conv_dw_b, conv_ln_g, conv_ln_b, w_conv_o,
           gdn_conv_w, gdn_A_log, gdn_dt_bias, gdn_norm_g, w_gdn_o, w_out, ln_g, ln_b,
           peer_wq, peer_keys, peer_u, peer_v):
    b_ctx, seq, d = x_prompt.shape
    b_lat, lat_seq, _ = x_sample.shape
    depth = w_mod.shape[0]
    past = cache_attn_k.shape[2]
    n_ctx_tok = b_ctx * seq
    alpha = (2 * depth) ** 0.25

    sizes = np.cumsum([0, 512, 512, 512, 1024, 1536, 512, 16, 3072])
    aq0, ak0, av0, cin0, gqkv0, gz0, gab0, mg0, end = [int(s) for s in sizes]
    w_p = jnp.concatenate([
        w_in[:, :, mg0:end], w_in[:, :, cin0:gqkv0], w_in[:, :, aq0:cin0], w_in[:, :, gz0:gab0],
        w_in[:, :, gqkv0:gz0], w_in[:, :, gab0:mg0],
        jnp.zeros((depth, d, N_PROJ - int(end)), w_in.dtype)], axis=-1).astype(BF16)
    wa_b, wc_b, wg_b, wo_b = (w.astype(BF16) for w in (w_attn_o, w_conv_o, w_gdn_o, w_out))
    wqt_b = jnp.swapaxes(peer_wq, 1, 2).astype(BF16)
    keys_b = peer_keys.astype(BF16)
    u_b = peer_u.astype(BF16)
    v_b = peer_v.astype(BF16)
    conv_w_p = jnp.pad(conv_dw_w, ((0, 0), (0, 32 - CONV_W), (0, 0)))
    rope_tabs = _rope_tables(lat_seq)
    cache_k = cache_attn_k.reshape(b_lat, depth, past, A_HEADS * LANES)
    cache_v = cache_attn_v.reshape(b_lat, depth, past, A_HEADS * LANES)

    cond8 = jnp.concatenate([c_ctx[None, :], c, jnp.zeros((SUBLANES - 1 - b_lat, d), F32)], axis=0)
    mod_all = _mod_call(cond8, w_mod, b_mod)

    x = jnp.concatenate([x_prompt.reshape(n_ctx_tok, d), x_sample.reshape(b_lat * lat_seq, d)], axis=0)
    k_list, v_list, s_list = [], [], []
    for l in range(depth):
        mod_l = mod_all[l]
        lam_init = 0.8 - 0.6 * math.exp(-0.3 * l)
        proj = _inproj_call(x, mod_l, w_p[l], n_ctx_tok, lat_seq)
        k_list.append(proj[:n_ctx_tok, OFF_AK:OFF_AK + 512].reshape(b_ctx, seq, A_HEADS, 2 * A_DQK))
        v_list.append(proj[:n_ctx_tok, OFF_AV:OFF_AV + 512].reshape(b_ctx, seq, A_HEADS, A_DV))

        dl = diff_lambda[l]
        ng_a = diff_norm_g[l].reshape(1, LANES)
        oa_ctx = _attn_ctx_call(proj, dl, ng_a, b_ctx, seq, lam_init)
        oa_lat = _attn_lat_call(proj, cache_k, cache_v, l, rope_tabs, dl, ng_a, n_ctx_tok, b_lat, lat_seq,
                                lam_init)

        cvec = lambda a: a[l].reshape(1, C_CONV)
        cargs = (conv_w_p[l], cvec(conv_dw_b), cvec(conv_ln_g), cvec(conv_ln_b))
        oc_ctx = _conv_call(proj, *cargs, 0, b_ctx, seq)
        oc_lat = _conv_call(proj, *cargs, n_ctx_tok // lat_seq, b_lat, lat_seq)

        gargs = (gdn_conv_w[l], _lane_vec(gdn_A_log[l]), _lane_vec(gdn_dt_bias[l]),
                 gdn_norm_g[l].reshape(1, LANES))
        og_ctx, s_new = _gdn_call(proj, *gargs, None, 0, b_ctx, seq, GDN_HEADS_CTX, True)
        og_lat, _ = _gdn_call(proj, *gargs, state_gdn[:, l], n_ctx_tok // lat_seq, b_lat, lat_seq,
                               GDN_HEADS_LAT, False)
        s_list.append(s_new)

        branches = (oa_ctx, oa_lat, oc_ctx, oc_lat, og_ctx, og_lat)
        x1, h2 = _merge_call(x, proj, branches, wa_b[l], wc_b[l], wg_b[l], wo_b[l], mod_l,
                             ln_g[l, 0:1], ln_b[l, 0:1], n_ctx_tok, lat_seq, alpha)
        x = _peer_call(h2, x1, wqt_b[l], keys_b[l], u_b[l], v_b[l], mod_l, ln_g[l, 1:2], ln_b[l, 1:2],
                       n_ctx_tok, lat_seq, alpha)

    y_prompt = x[:n_ctx_tok].reshape(b_ctx, seq, d)
    y_sample = x[n_ctx_tok:].reshape(b_lat, lat_seq, d)
    return (y_prompt, y_sample, jnp.stack(k_list, axis=1), jnp.stack(v_list, axis=1),
            jnp.stack(s_list, axis=1))
```

```python
import functools
import math

import numpy as np
import jax
import jax.numpy as jnp
from jax import lax
from jax.experimental import pallas as pl
from jax.experimental.pallas import tpu as pltpu

F32 = jnp.float32
BF16 = jnp.bfloat16

LANES = 128
SUBLANES = 8
VMEM_LIMIT = 56 * 1024 * 1024

D_MODEL = 1024
A_HEADS = 4
A_DQK = 64
A_DV = 128
GRID_W = 64
ROPE_BASE = 10000.0
C_CONV = 512
CONV_W = 31
G_HEADS = 4
G_DK = 128
G_SHORT = 3
G_CHUNK = 128
P_HEADS = 8
N_KEYS = 128
P_TOPK = 16
LN_EPS = 1e-5

N_PROJ = 8192
OFF_MG, OFF_CA, OFF_CB, OFF_AQ, OFF_AK, OFF_AV, OFF_GZ, OFF_GQ, OFF_GK, OFF_GV, OFF_GAB = (
    0, 3072, 3584, 4096, 4608, 5120, 5632, 6144, 6656, 7168, 7680)

TB_PROJ = 512
TN_PROJ = 2048
TB_MERGE = 256
TB_PEER = 512
EB_PEER = 1024


def _cparams(*sem):
    return pltpu.CompilerParams(dimension_semantics=sem, vmem_limit_bytes=VMEM_LIMIT)


def _mm(a, b):
    return jnp.dot(a.astype(BF16), b.astype(BF16), preferred_element_type=F32)


def _mm_nt(a, b):
    return lax.dot_general(a.astype(BF16), b.astype(BF16), (((1,), (1,)), ((), ())),
                           preferred_element_type=F32)


def _silu(x):
    return x * jax.nn.sigmoid(x)


def _mod_row(i, n_ctx_blocks, blocks_per_lat):
    return jnp.where(i < n_ctx_blocks, 0, 1 + (i - n_ctx_blocks) // blocks_per_lat)


def _layer_norm(x, g, b):
    mu = jnp.mean(x, axis=-1, keepdims=True)
    xc = x - mu
    var = jnp.mean(xc * xc, axis=-1, keepdims=True)
    return xc * lax.rsqrt(var + LN_EPS) * g + b


def _mod_kernel(cond_ref, w_ref, b_ref, o_ref):
    a = _silu(cond_ref[...])
    o_ref[0] = _mm(a, w_ref[0]) + b_ref[0]


def _mod_call(cond8, w_mod, b_mod):
    depth, d, n = w_mod.shape
    tn = 1536
    return pl.pallas_call(
        _mod_kernel,
        grid=(depth, n // tn),
        in_specs=[pl.BlockSpec((SUBLANES, d), lambda l, j: (0, 0)),
                  pl.BlockSpec((1, d, tn), lambda l, j: (l, 0, j)),
                  pl.BlockSpec((1, 1, tn), lambda l, j: (l, 0, j))],
        out_specs=pl.BlockSpec((1, SUBLANES, tn), lambda l, j: (l, 0, j)),
        out_shape=jax.ShapeDtypeStruct((depth, SUBLANES, n), F32),
        compiler_params=_cparams("arbitrary", "arbitrary"),
    )(cond8, w_mod, b_mod.reshape(depth, 1, n))


def _inproj_kernel(x_ref, mod_ref, w_ref, o_ref, *, n_ctx_blocks, blocks_per_lat):
    row = _mod_row(pl.program_id(1), n_ctx_blocks, blocks_per_lat)
    sh = mod_ref[pl.ds(row, 1), 0:D_MODEL]
    sc = mod_ref[pl.ds(row, 1), D_MODEL:2 * D_MODEL]
    h = x_ref[...] * (1.0 + sc) + sh
    o_ref[...] = _mm(h, w_ref[...])


def _inproj_call(x, mod_l, w_p, n_ctx_tok, lat_seq):
    n_tok = x.shape[0]
    kern = functools.partial(_inproj_kernel, n_ctx_blocks=n_ctx_tok // TB_PROJ,
                             blocks_per_lat=lat_seq // TB_PROJ)
    return pl.pallas_call(
        kern,
        grid=(N_PROJ // TN_PROJ, n_tok // TB_PROJ),
        in_specs=[pl.BlockSpec((TB_PROJ, D_MODEL), lambda j, i: (i, 0)),
                  pl.BlockSpec((SUBLANES, 6 * D_MODEL), lambda j, i: (0, 0)),
                  pl.BlockSpec((D_MODEL, TN_PROJ), lambda j, i: (0, j))],
        out_specs=pl.BlockSpec((TB_PROJ, TN_PROJ), lambda j, i: (i, j)),
        out_shape=jax.ShapeDtypeStruct((n_tok, N_PROJ), F32),
        compiler_params=_cparams("arbitrary", "arbitrary"),
    )(x, mod_l, w_p)


def _softmax(s):
    e = jnp.exp(s - jnp.max(s, axis=-1, keepdims=True))
    return e / jnp.sum(e, axis=-1, keepdims=True)


def _diff_lambda(dl_ref, lam_init):
    dl = dl_ref[...]
    a = jnp.sum(dl[0:1] * dl[1:2], axis=-1, keepdims=True)
    b = jnp.sum(dl[2:3] * dl[3:4], axis=-1, keepdims=True)
    return jnp.exp(a) - jnp.exp(b) + lam_init


def _diff_attn_heads(q, k_of, v_of, lam, ng, lam_init, o_ref):
    lo = lax.broadcasted_iota(jnp.int32, (1, LANES), 1) < A_DQK
    scale = A_DQK ** -0.5
    for h in range(A_HEADS):
        qh = q[:, h * LANES:(h + 1) * LANES]
        kh = k_of(h)
        s1 = _mm_nt(jnp.where(lo, qh, 0.0), kh) * scale
        s2 = _mm_nt(jnp.where(lo, 0.0, qh), kh) * scale
        a = _softmax(s1) - lam * _softmax(s2)
        o = _mm(a, v_of(h))
        o = o * lax.rsqrt(jnp.mean(o * o, axis=-1, keepdims=True) + LN_EPS) * ng
        o_ref[:, h * LANES:(h + 1) * LANES] = (o * (1.0 - lam_init)).astype(o_ref.dtype)


def _attn_ctx_kernel(*refs, lam_init):
    q_ref, k_ref, v_ref, dl_ref, ng_ref = refs[:5]
    o_ref, ko_ref, vo_ref = refs[-3:]
    ko_ref[0, 0] = k_ref[...]
    vo_ref[0, 0] = v_ref[...]
    lam = _diff_lambda(dl_ref, lam_init)
    _diff_attn_heads(q_ref[...],
                     lambda h: k_ref[:, h * LANES:(h + 1) * LANES].astype(BF16),
                     lambda h: v_ref[:, h * LANES:(h + 1) * LANES].astype(BF16),
                     lam, ng_ref[...], lam_init, o_ref)


def _attn_ctx_call(proj, dl, ng, n_seq, seq, lam_init, layer, depth, caches):
    w = A_HEADS * LANES
    spec = lambda off: pl.BlockSpec((seq, w), lambda b, off=off: (b, off // w))
    in_specs = [spec(OFF_AQ), spec(OFF_AK), spec(OFF_AV),
                pl.BlockSpec(dl.shape, lambda b: (0, 0)),
                pl.BlockSpec((1, LANES), lambda b: (0, 0))]
    args = [proj, proj, proj, dl, ng]
    aliases = {}
    if caches is not None:
        aliases = {len(args): 1, len(args) + 1: 2}
        in_specs += [pl.BlockSpec(memory_space=pl.ANY)] * 2
        args += list(caches)
    cache_spec = pl.BlockSpec((1, 1, seq, w), lambda b: (b, layer, 0, 0))
    cache_shape = jax.ShapeDtypeStruct((n_seq, depth, seq, w), F32)
    return pl.pallas_call(
        functools.partial(_attn_ctx_kernel, lam_init=lam_init),
        grid=(n_seq,),
        in_specs=in_specs,
        out_specs=[pl.BlockSpec((seq, w), lambda b: (b, 0)), cache_spec, cache_spec],
        out_shape=[jax.ShapeDtypeStruct((n_seq * seq, w), BF16), cache_shape, cache_shape],
        input_output_aliases=aliases,
        compiler_params=_cparams("arbitrary"),
    )(*args)


def _rope(x, cos, sin_lo, sin_hi):
    n = x.shape[-1]
    return x * cos + pltpu.roll(x, n - 16, 1) * sin_lo + pltpu.roll(x, 16, 1) * sin_hi


def _attn_lat_kernel(q_ref, k_ref, v_ref, ck_ref, cv_ref, cosq_ref, slq_ref, shq_ref,
                     cosk_ref, slk_ref, shk_ref, dl_ref, ng_ref, o_ref, kall_ref, vall_ref,
                     *, lam_init, past):
    @pl.when(pl.program_id(1) == 0)
    def _():
        kall_ref[0:past, :] = ck_ref[0, 0].astype(BF16)
        vall_ref[0:past, :] = cv_ref[0, 0].astype(BF16)
        kall_ref[past:, :] = _rope(k_ref[...], cosk_ref[...], slk_ref[...], shk_ref[...]).astype(BF16)
        vall_ref[past:, :] = v_ref[...].astype(BF16)

    lam = _diff_lambda(dl_ref, lam_init)
    q = _rope(q_ref[...], cosq_ref[...], slq_ref[...], shq_ref[...])
    _diff_attn_heads(q,
                     lambda h: kall_ref[:, h * LANES:(h + 1) * LANES],
                     lambda h: vall_ref[:, h * LANES:(h + 1) * LANES],
                     lam, ng_ref[...], lam_init, o_ref)


def _attn_lat_call(proj, cache_k, cache_v, layer, rope_tabs, dl, ng, n_ctx_tok, n_seq, seq, lam_init):
    w = A_HEADS * LANES
    past = cache_k.shape[2]
    qb = 256
    nqb = seq // qb
    row0 = n_ctx_tok // seq
    rowq0 = n_ctx_tok // qb
    cos, s_lo, s_hi = rope_tabs
    qspec = pl.BlockSpec((qb, w), lambda b, i: (rowq0 + b * nqb + i, OFF_AQ // w))
    kspec = pl.BlockSpec((seq, w), lambda b, i: (row0 + b, OFF_AK // w))
    vspec = pl.BlockSpec((seq, w), lambda b, i: (row0 + b, OFF_AV // w))
    cspec = pl.BlockSpec((1, 1, past, w), lambda b, i: (b, layer, 0, 0))
    tq = pl.BlockSpec((qb, w), lambda b, i: (i, 0))
    tk = pl.BlockSpec((seq, w), lambda b, i: (0, 0))
    return pl.pallas_call(
        functools.partial(_attn_lat_kernel, lam_init=lam_init, past=past),
        grid=(n_seq, nqb),
        in_specs=[qspec, kspec, vspec, cspec, cspec, tq, tq, tq, tk, tk, tk,
                  pl.BlockSpec(dl.shape, lambda b, i: (0, 0)),
                  pl.BlockSpec((1, LANES), lambda b, i: (0, 0))],
        out_specs=pl.BlockSpec((qb, w), lambda b, i: (b * nqb + i, 0)),
        out_shape=jax.ShapeDtypeStruct((n_seq * seq, w), BF16),
        scratch_shapes=[pltpu.VMEM((past + seq, w), BF16), pltpu.VMEM((past + seq, w), BF16)],
        compiler_params=_cparams("arbitrary", "arbitrary"),
    )(proj, proj, proj, cache_k, cache_v, cos, s_lo, s_hi, cos, s_lo, s_hi, dl, ng)


CONV_PAD = 16
CONV_ROWS = 64


def _conv_kernel(ca_ref, cb_ref, w_ref, b_ref, lg_ref, lb_ref, o_ref, pad_ref, *, seq):
    zeros = jnp.zeros((CONV_PAD, C_CONV), F32)
    pad_ref[0:CONV_PAD, :] = zeros
    pad_ref[seq + CONV_PAD:seq + 2 * CONV_PAD, :] = zeros
    pad_ref[CONV_PAD:seq + CONV_PAD, :] = ca_ref[...] * jax.nn.sigmoid(cb_ref[...])
    half = CONV_W // 2

    def body(ci, carry):
        base = pl.multiple_of(ci * CONV_ROWS, CONV_ROWS)
        win = pad_ref[pl.ds(base, CONV_ROWS + 2 * CONV_PAD), :]
        acc = jnp.zeros((CONV_ROWS, C_CONV), F32)
        for r in range(SUBLANES):
            shifted = win[r:r + CONV_ROWS + 3 * SUBLANES]
            for qq in range(4):
                j = qq * SUBLANES + r - (CONV_PAD - half)
                if 0 <= j < CONV_W:
                    acc = acc + w_ref[j:j + 1, :] * shifted[qq * SUBLANES:qq * SUBLANES + CONV_ROWS]
        y = _layer_norm(acc + b_ref[...], lg_ref[...], lb_ref[...])
        o_ref[pl.ds(base, CONV_ROWS), :] = _silu(y).astype(o_ref.dtype)
        return carry

    lax.fori_loop(0, seq // CONV_ROWS, body, 0)


def _conv_call(proj, w, b, lg, lb, row0, n_seq, seq):
    spec = lambda off: pl.BlockSpec((seq, C_CONV), lambda s, off=off: (row0 + s, off // C_CONV))
    vec = pl.BlockSpec((1, C_CONV), lambda s: (0, 0))
    return pl.pallas_call(
        functools.partial(_conv_kernel, seq=seq),
        grid=(n_seq,),
        in_specs=[spec(OFF_CA), spec(OFF_CB), pl.BlockSpec(w.shape, lambda s: (0, 0)), vec, vec, vec],
        out_specs=pl.BlockSpec((seq, C_CONV), lambda s: (s, 0)),
        out_shape=jax.ShapeDtypeStruct((n_seq * seq, C_CONV), BF16),
        scratch_shapes=[pltpu.VMEM((seq + 2 * CONV_PAD, C_CONV), F32)],
        compiler_params=_cparams("arbitrary"),
    )(proj, proj, w, b, lg, lb)


GDN_GROUP = 8
GDN_HEADS_CTX = 4
GDN_HEADS_LAT = 2


def _split(a):
    hi = a.astype(BF16)
    return hi, (a - hi.astype(F32)).astype(BF16)


def _mm_split(a, b):
    (ah, al), (bh, bl) = a, b
    n = bh.shape[1]
    lhs = jnp.concatenate([ah, al], axis=1)
    rhs = jnp.concatenate([jnp.concatenate([bh, bl], axis=1),
                           jnp.concatenate([bh, jnp.zeros_like(bl)], axis=1)], axis=0)
    out = jnp.dot(lhs, rhs, preferred_element_type=F32)
    return out[:, :n] + out[:, n:]


def _unit_tri_inverses(l_mats):
    n = l_mats[0].shape[0]
    ri = lax.broadcasted_iota(jnp.int32, (n, n), 0)
    ci = lax.broadcasted_iota(jnp.int32, (n, n), 1)
    same_block = lambda size: (ri // size) == (ci // size)
    eye = (ri == ci).astype(F32)
    diag = [jnp.where(same_block(SUBLANES), l, 0.0) for l in l_mats]
    ps = [eye - d for d in diag]
    ms = [_split(d) for d in diag]
    for _ in range(int(math.log2(SUBLANES)) - 1):
        ms = [_split(_mm_split(m, m)) for m in ms]
        ps = [p + _mm_split(_split(p), m) for p, m in zip(ps, ms)]
    size = SUBLANES
    while size < n:
        off = same_block(2 * size) & jnp.logical_not(same_block(size))
        new_ps = []
        for p, l in zip(ps, l_mats):
            p_s = _split(p)
            t = _mm_split(_split(jnp.where(off, l, 0.0)), p_s)
            new_ps.append(p - _mm_split(p_s, _split(t)))
        ps = new_ps
        size *= 2
    return ps


def _gdn_kernel(*refs, seq, heads, has_s0, emit_state):
    (q_ref, k_ref, v_ref, gab_ref, z_ref, cwq_ref, cwk_ref, cwv_ref, al_ref, dt_ref, ng_ref) = refs[:11]
    idx = 11
    s0_ref = None
    if has_s0:
        s0_ref = refs[idx]
        idx += 1
    o_ref = refs[idx]
    idx += 1
    sn_ref = None
    if emit_state:
        sn_ref = refs[idx]
        idx += 1
    (pad_ref, kc_s, kt_s, qc_s, vc_s, gcb_s, btb_s, grow_s, w_s, u_s, qk_s, cumt_ref, oacc_s) = refs[idx:]

    c = G_CHUNK
    n_chunks = seq // c
    head0 = pl.program_id(1) * heads
    hsl = lambda hl: slice(hl * LANES, (hl + 1) * LANES)

    zeros8 = jnp.zeros((SUBLANES, heads * LANES), F32)
    pad_ref[0:SUBLANES, :] = zeros8
    pad_ref[seq + SUBLANES:seq + 2 * SUBLANES, :] = zeros8

    def short_conv(x_ref, w_ref):
        pad_ref[SUBLANES:seq + SUBLANES, :] = x_ref[...]
        y = (w_ref[0:1, :] * pad_ref[SUBLANES - 1:seq + SUBLANES - 1, :]
             + w_ref[1:2, :] * pad_ref[SUBLANES:seq + SUBLANES, :]
             + w_ref[2:3, :] * pad_ref[SUBLANES + 1:seq + SUBLANES + 1, :])
        return _silu(y)

    q = short_conv(q_ref, cwq_ref)
    k = short_conv(k_ref, cwk_ref)
    v = short_conv(v_ref, cwv_ref)

    gab = gab_ref[...]
    xg = gab + dt_ref[...]
    softplus = jnp.maximum(xg, 0.0) + jnp.log1p(jnp.exp(-jnp.abs(xg)))
    g_all = -jnp.exp(al_ref[...]) * softplus
    b_all = jax.nn.sigmoid(gab)

    g_t = g_all.T
    pos = lax.broadcasted_iota(jnp.int32, (1, seq), 1) % c
    yf = g_t
    yb = g_t
    s = 1
    while s < c:
        yf = yf + jnp.where(pos >= s, pltpu.roll(yf, s, 1), 0.0)
        yb = yb + jnp.where(pos < c - s, pltpu.roll(yb, seq - s, 1), 0.0)
        s *= 2
    rowi = lax.broadcasted_iota(jnp.int32, (LANES, 1), 0)
    cum_t = jnp.where(rowi < G_HEADS, yf, yb)
    cumt_ref[...] = cum_t
    cum_c = cum_t.T
    lane = lax.broadcasted_iota(jnp.int32, (1, LANES), 1)

    for hl in range(heads):
        qh = q[:, hsl(hl)]
        kh = k[:, hsl(hl)]
        qn = qh * lax.rsqrt(jnp.sum(qh * qh, axis=-1, keepdims=True) + 1e-6) * (G_DK ** -0.5)
        kn = kh * lax.rsqrt(jnp.sum(kh * kh, axis=-1, keepdims=True) + 1e-6)
        for n in range(n_chunks):
            sl = slice(n * c, (n + 1) * c)
            kc_s[hl, n] = kn[sl]
            kt_s[hl, n] = kn[sl].T
            qc_s[hl, n] = qn[sl]
            vc_s[hl, n] = v[sl, hsl(hl)]
        for d in range(2):
            r = d * G_HEADS + head0 + hl
            gcol = jnp.sum(jnp.where(lane == r, cum_c, 0.0), axis=1, keepdims=True)
            bcol = jnp.sum(jnp.where(lane == 2 * G_HEADS + r, b_all, 0.0), axis=1, keepdims=True)
            grow = cumt_ref[pl.ds(r, 1), :]
            for n in range(n_chunks):
                sl = slice(n * c, (n + 1) * c)
                gcb_s[hl, d, n] = jnp.broadcast_to(gcol[sl], (c, LANES))
                btb_s[hl, d, n] = jnp.broadcast_to(bcol[sl], (c, LANES))
                grow_s[hl, d, n] = jnp.broadcast_to(grow[:, sl], (SUBLANES, c))

    ri = lax.broadcasted_iota(jnp.int32, (c, c), 0)
    ci = lax.broadcasted_iota(jnp.int32, (c, c), 1)
    incl = [ri >= ci, ri <= ci]
    strict = [ri > ci, ri < ci]

    instances = [(hl, d, n) for n in range(n_chunks) for d in range(2) for hl in range(heads)]
    for g0 in range(0, len(instances), GDN_GROUP):
        group = instances[g0:g0 + GDN_GROUP]
        l_mats, rhs_w, rhs_u = [], [], []
        for hl, d, n in group:
            kt = kt_s[hl, n]
            gcb = gcb_s[hl, d, n]
            btb = btb_s[hl, d, n]
            decay = jnp.exp(jnp.where(incl[d], gcb - grow_s[hl, d, n][0:1, :], -jnp.inf))
            kbeta = kc_s[hl, n] * btb
            l_mats.append(jnp.where(strict[d], _mm(kbeta, kt) * decay, 0.0))
            rhs_w.append(_split(kbeta * jnp.exp(gcb)))
            rhs_u.append(_split(vc_s[hl, n] * btb))
            qk_s[hl, d, n] = jnp.where(incl[d], _mm(qc_s[hl, n], kt) * decay, 0.0)
        invs = [_split(p) for p in _unit_tri_inverses(l_mats)]
        for (hl, d, n), inv, rw, ru in zip(group, invs, rhs_w, rhs_u):
            w_s[hl, d, n] = _mm_split(inv, rw)
            u_s[hl, d, n] = _mm_split(inv, ru)

    states = [[s0_ref[0, d, hl] if has_s0 else jnp.zeros((G_DK, LANES), F32) for d in range(2)]
              for hl in range(heads)]
    for i in range(n_chunks):
        for hl in range(heads):
            for d in range(2):
                n = i if d == 0 else n_chunks - 1 - i
                gcb = gcb_s[hl, d, n]
                glast = gcb[c - 1:c, :] if d == 0 else gcb[0:1, :]
                state = states[hl][d]
                v_new = u_s[hl, d, n] - _mm(w_s[hl, d, n], state)
                oacc_s[hl, d, n] = _mm(qc_s[hl, n] * jnp.exp(gcb), state) + _mm(qk_s[hl, d, n], v_new)
                kdec_t = kt_s[hl, n] * jnp.exp(glast - grow_s[hl, d, n][0:1, :])
                states[hl][d] = state * jnp.exp(glast) + _mm(kdec_t, v_new)
    if emit_state:
        for hl in range(heads):
            for d in range(2):
                sn_ref[0, d, hl] = states[hl][d]

    ng = ng_ref[...]
    for hl in range(heads):
        for n in range(n_chunks):
            sl = slice(n * c, (n + 1) * c)
            o = oacc_s[hl, 0, n] + oacc_s[hl, 1, n]
            o = o * lax.rsqrt(jnp.mean(o * o, axis=-1, keepdims=True) + LN_EPS) * ng
            o_ref[sl, hsl(hl)] = (o * _silu(z_ref[sl, hsl(hl)])).astype(o_ref.dtype)


def _gdn_call(proj, cw, al, dt, ng, s0, row0, n_seq, seq, heads, emit_state):
    n_chunks = seq // G_CHUNK
    w = heads * LANES
    n_hb = G_HEADS // heads
    col = lambda off: pl.BlockSpec((seq, w), lambda b, h, off=off: (row0 + b, off // w + h))
    cwspec = lambda part: pl.BlockSpec((G_SHORT, w), lambda b, h, part=part: (0, part * n_hb + h))
    vec = pl.BlockSpec((1, LANES), lambda b, h: (0, 0))
    in_specs = [col(OFF_GQ), col(OFF_GK), col(OFF_GV),
                pl.BlockSpec((seq, LANES), lambda b, h: (row0 + b, OFF_GAB // LANES)),
                col(OFF_GZ), cwspec(0), cwspec(1), cwspec(2), vec, vec, vec]
    args = [proj, proj, proj, proj, proj, cw, cw, cw, al, dt, ng]
    state_spec = pl.BlockSpec((1, 2, heads, G_DK, LANES), lambda b, h: (b, 0, h, 0, 0))
    if s0 is not None:
        in_specs.append(state_spec)
        args.append(s0)
    out_specs = [pl.BlockSpec((seq, w), lambda b, h: (b, h))]
    out_shape = [jax.ShapeDtypeStruct((n_seq * seq, G_HEADS * LANES), BF16)]
    if emit_state:
        out_specs.append(state_spec)
        out_shape.append(jax.ShapeDtypeStruct((n_seq, 2, G_HEADS, G_DK, LANES), F32))
    chunked = lambda lead: pltpu.VMEM(lead + (G_CHUNK, LANES), F32)
    scratch = [pltpu.VMEM((seq + 2 * SUBLANES, w), F32),
               chunked((heads, n_chunks)), chunked((heads, n_chunks)), chunked((heads, n_chunks)),
               chunked((heads, n_chunks)),
               chunked((heads, 2, n_chunks)), chunked((heads, 2, n_chunks)),
               pltpu.VMEM((heads, 2, n_chunks, SUBLANES, G_CHUNK), F32),
               chunked((heads, 2, n_chunks)), chunked((heads, 2, n_chunks)), chunked((heads, 2, n_chunks)),
               pltpu.VMEM((LANES, seq), F32), chunked((heads, 2, n_chunks))]
    res = pl.pallas_call(
        functools.partial(_gdn_kernel, seq=seq, heads=heads, has_s0=s0 is not None, emit_state=emit_state),
        grid=(n_seq, n_hb),
        in_specs=in_specs,
        out_specs=out_specs,
        out_shape=out_shape,
        scratch_shapes=scratch,
        compiler_params=_cparams("arbitrary", "arbitrary"),
    )(*args)
    return res if emit_state else (res[0], None)


def _merge_kernel(x_ref, mg0_ref, mg1_ref, mg2_ref, oac_ref, oal_ref, occ_ref, ocl_ref, ogc_ref, ogl_ref,
                  wa_ref, wc_ref, wg_ref, wo_ref, mod_ref, lng_ref, lnb_ref, x1_ref, h2_ref,
                  *, n_ctx_blocks, blocks_per_lat, alpha):
    i = pl.program_id(0)
    row = _mod_row(i, n_ctx_blocks, blocks_per_lat)
    g1 = mod_ref[pl.ds(row, 1), 2 * D_MODEL:3 * D_MODEL]
    sh2 = mod_ref[pl.ds(row, 1), 3 * D_MODEL:4 * D_MODEL]
    sc2 = mod_ref[pl.ds(row, 1), 4 * D_MODEL:5 * D_MODEL]
    is_ctx = i < n_ctx_blocks
    pick = lambda c_ref, l_ref: jnp.where(is_ctx, c_ref[...], l_ref[...])
    merged = (jax.nn.sigmoid(mg0_ref[...]) * _mm(pick(oac_ref, oal_ref), wa_ref[...])
              + jax.nn.sigmoid(mg1_ref[...]) * _mm(pick(occ_ref, ocl_ref), wc_ref[...])
              + jax.nn.sigmoid(mg2_ref[...]) * _mm(pick(ogc_ref, ogl_ref), wg_ref[...]))
    mix = _mm(merged, wo_ref[...])
    x1 = _layer_norm(alpha * x_ref[...] + g1 * mix, lng_ref[...], lnb_ref[...])
    x1_ref[...] = x1
    h2_ref[...] = (x1 * (1.0 + sc2) + sh2).astype(h2_ref.dtype)


def _merge_call(x, proj, branches, wa, wc, wg, wo, mod_l, lng, lnb, n_ctx_tok, lat_seq, alpha):
    n_tok = x.shape[0]
    tb = TB_MERGE
    n_cb = n_ctx_tok // tb
    tok = lambda w: pl.BlockSpec((tb, w), lambda i: (i, 0))
    mg = lambda j: pl.BlockSpec((tb, D_MODEL), lambda i, j=j: (i, j))
    full = lambda a: pl.BlockSpec(a.shape, lambda i: (0, 0))
    ctx = pl.BlockSpec((tb, 512), lambda i: (jnp.minimum(i, n_cb - 1), 0))
    lat = pl.BlockSpec((tb, 512), lambda i: (jnp.maximum(i - n_cb, 0), 0))
    kern = functools.partial(_merge_kernel, n_ctx_blocks=n_cb, blocks_per_lat=lat_seq // tb, alpha=alpha)
    return pl.pallas_call(
        kern,
        grid=(n_tok // tb,),
        in_specs=[tok(D_MODEL), mg(0), mg(1), mg(2), ctx, lat, ctx, lat, ctx, lat,
                  full(wa), full(wc), full(wg), full(wo), full(mod_l), full(lng), full(lnb)],
        out_specs=[tok(D_MODEL), tok(D_MODEL)],
        out_shape=[jax.ShapeDtypeStruct((n_tok, D_MODEL), F32), jax.ShapeDtypeStruct((n_tok, D_MODEL), BF16)],
        compiler_params=_cparams("arbitrary"),
    )(x, proj, proj, proj, *branches, wa, wc, wg, wo, mod_l, lng, lnb)


def _top16(s):
    rowi = lax.broadcasted_iota(jnp.int32, (P_TOPK, 1), 0)
    out = jnp.zeros((P_TOPK, s.shape[1]), F32)
    w = s
    for it in range(P_TOPK):
        m = jnp.max(w, axis=0, keepdims=True)
        out = jnp.where(rowi == it, m, out)
        if it + 1 < P_TOPK:
            w = jnp.where(w >= m, -jnp.inf, w)
    return out


def _peer_select(s1, s2):
    sv1 = _top16(s1)
    sv2 = _top16(s2)
    m1 = sv1[0:1]
    m2 = sv2[0:1]
    ea = jnp.exp(sv1 - m1)
    eb = jnp.exp(sv2 - m2)
    lo, hi = slice(0, SUBLANES), slice(SUBLANES, P_TOPK)
    cands = [sv1[lo] + sv2[0:1], sv1[hi] + sv2[0:1]]
    wts = [ea[lo] * eb[0:1], ea[hi] * eb[0:1]]
    for b in range(1, SUBLANES):
        cands.append(sv1[lo] + sv2[b:b + 1])
        wts.append(ea[lo] * eb[b:b + 1])
    cands.append(sv2[hi] + sv1[0:1])
    wts.append(eb[hi] * ea[0:1])
    work = list(cands)
    tau = None
    for it in range(P_TOPK):
        tau = jnp.max(functools.reduce(jnp.maximum, work), axis=0, keepdims=True)
        if it + 1 < P_TOPK:
            work = [jnp.where(w >= tau, -jnp.inf, w) for w in work]
    z = jnp.sum(functools.reduce(lambda a, b: a + b,
                                 [jnp.where(cd >= tau, wt, 0.0) for cd, wt in zip(cands, wts)]),
                axis=0, keepdims=True)
    thr = jnp.full(s1.shape, jnp.inf, F32)
    for b in range(P_TOPK):
        thr = jnp.where(s1 + sv2[b:b + 1] >= tau, sv2[b:b + 1], thr)
    e1 = jnp.exp(s1 - m1) * (1.0 / z)
    e2 = jnp.exp(s2 - m2)
    return thr, e1, e2


def _peer_kernel(h2_ref, x1_ref, wqt_ref, keys_ref, u_ref, v_ref, mod_ref, lng_ref, lnb_ref, o_ref,
                 qt_ref, s_ref, e_ref, thr_ref, ga_ref, acc_ref, *, n_ctx_blocks, blocks_per_lat, alpha):
    j = pl.program_id(1)
    last = pl.num_programs(1) - 1
    row = _mod_row(pl.program_id(0), n_ctx_blocks, blocks_per_lat)
    tb = TB_PEER
    n_col = tb // LANES

    @pl.when(j == 0)
    def _():
        qt_ref[...] = _mm_nt(wqt_ref[...], h2_ref[...])
        for hh in range(P_HEADS):
            for p in range(2):
                r0 = (hh * 2 + p) * N_KEYS
                s_ref[hh, p] = _mm(keys_ref[p], qt_ref[r0:r0 + N_KEYS, :])
        for cc in range(n_col):
            cs = slice(cc * LANES, (cc + 1) * LANES)
            for hh in range(P_HEADS):
                thr, e1, e2 = _peer_select(s_ref[hh, 0, :, cs], s_ref[hh, 1, :, cs])
                thr_ref[hh, :, cs] = thr
                e_ref[hh, 0, :, cs] = e1
                e_ref[hh, 1, :, cs] = e2
        acc_ref[...] = jnp.zeros_like(acc_ref)

    act = _mm_nt(u_ref[...], h2_ref[...])
    gel = 0.5 * act * (1.0 + lax.erf(act * (2.0 ** -0.5)))
    i1_base = pl.multiple_of(j * SUBLANES, SUBLANES)
    for cc in range(n_col):
        cs = slice(cc * LANES, (cc + 1) * LANES)
        for half in range(EB_PEER // N_KEYS):
            g = jnp.zeros((N_KEYS, LANES), F32)
            for hh in range(P_HEADS):
                thr_row = thr_ref[hh, pl.ds(i1_base, SUBLANES), cs][half:half + 1]
                e1row = e_ref[hh, 0, pl.ds(i1_base, SUBLANES), cs][half:half + 1]
                sel = s_ref[hh, 1, :, cs] >= thr_row
                g = g + jnp.where(sel, e1row * e_ref[hh, 1, :, cs], 0.0)
            ga_ref[half * N_KEYS:(half + 1) * N_KEYS, cs] = (
                g * gel[half * N_KEYS:(half + 1) * N_KEYS, cs]).astype(BF16)
    acc_ref[...] += lax.dot_general(ga_ref[...], v_ref[...], (((0,), (0,)), ((), ())),
                                    preferred_element_type=F32)

    @pl.when(j == last)
    def _():
        g2 = mod_ref[pl.ds(row, 1), 5 * D_MODEL:6 * D_MODEL]
        o_ref[...] = _layer_norm(alpha * x1_ref[...] + g2 * acc_ref[...], lng_ref[...], lnb_ref[...])


def _peer_call(h2, x1, wqt, keys, u_tab, v_tab, mod_l, lng, lnb, n_ctx_tok, lat_seq, alpha):
    n_tok = h2.shape[0]
    tb = TB_PEER
    n_exp = u_tab.shape[0]
    tok = pl.BlockSpec((tb, D_MODEL), lambda i, j: (i, 0))
    full = lambda a: pl.BlockSpec(a.shape, lambda i, j: (0,) * a.ndim)
    tab = pl.BlockSpec((EB_PEER, D_MODEL), lambda i, j: (j, 0))
    kern = functools.partial(_peer_kernel, n_ctx_blocks=n_ctx_tok // tb, blocks_per_lat=lat_seq // tb,
                             alpha=alpha)
    return pl.pallas_call(
        kern,
        grid=(n_tok // tb, n_exp // EB_PEER),
        in_specs=[tok, tok, full(wqt), full(keys), tab, tab, full(mod_l), full(lng), full(lnb)],
        out_specs=tok,
        out_shape=jax.ShapeDtypeStruct((n_tok, D_MODEL), F32),
        scratch_shapes=[pltpu.VMEM((P_HEADS * 2 * N_KEYS, tb), F32),
                        pltpu.VMEM((P_HEADS, 2, N_KEYS, tb), F32),
                        pltpu.VMEM((P_HEADS, 2, N_KEYS, tb), F32),
                        pltpu.VMEM((P_HEADS, N_KEYS, tb), F32),
                        pltpu.VMEM((EB_PEER, tb), BF16),
                        pltpu.VMEM((tb, D_MODEL), F32)],
        compiler_params=_cparams("arbitrary", "arbitrary"),
    )(h2, x1, wqt, keys, u_tab, v_tab, mod_l, lng, lnb)


def _rope_tables(n_tokens):
    n_rows = n_tokens // GRID_W
    rows = np.repeat(np.arange(n_rows, dtype=np.float32), GRID_W)
    cols = np.tile(np.arange(GRID_W, dtype=np.float32), n_rows)
    half = A_DQK // 2
    inv_freq = (1.0 / (ROPE_BASE ** (jnp.arange(0, half, 2, dtype=F32) / half)))
    ang_r = jnp.asarray(rows)[:, None] * inv_freq
    ang_c = jnp.asarray(cols)[:, None] * inv_freq
    zeros = jnp.zeros_like(ang_r)
    cr, sr, cc, sc = jnp.cos(ang_r), jnp.sin(ang_r), jnp.cos(ang_c), jnp.sin(ang_c)
    cos = jnp.concatenate([cr, cr, cc, cc], axis=-1)
    s_lo = jnp.concatenate([-sr, zeros, -sc, zeros], axis=-1)
    s_hi = jnp.concatenate([zeros, sr, zeros, sc], axis=-1)
    reps = 2 * A_HEADS
    return tuple(jnp.tile(t, (1, reps)) for t in (cos, s_lo, s_hi))


def _lane_vec(a):
    flat = a.reshape(1, -1).astype(F32)
    return jnp.pad(flat, ((0, 0), (0, LANES - flat.shape[1])))


def kernel(x_prompt, x_sample, cache_attn_k, cache_attn_v, state_gdn, c, c_ctx, w_mod, b_mod, w_in,
           diff_lambda, diff_norm_g, w_attn_o, conv_dw_w, conv_dw_b, conv_ln_g, conv_ln_b, w_conv_o,
           gdn_conv_w, gdn_A_log, gdn_dt_bias, gdn_norm_g, w_gdn_o, w_out, ln_g, ln_b,
           peer_wq, peer_keys, peer_u, peer_v):
    b_ctx, seq, d = x_prompt.shape
    b_lat, lat_seq, _ = x_sample.shape
    depth = w_mod.shape[0]
    past = cache_attn_k.shape[2]
    n_ctx_tok = b_ctx * seq
    alpha = (2 * depth) ** 0.25

    sizes = np.cumsum([0, 512, 512, 512, 1024, 1536, 512, 16, 3072])
    aq0, ak0, av0, cin0, gqkv0, gz0, gab0, mg0, end = [int(s) for s in sizes]
    w_p = jnp.concatenate([
        w_in[:, :, mg0:end], w_in[:, :, cin0:gqkv0], w_in[:, :, aq0:cin0], w_in[:, :, gz0:gab0],
        w_in[:, :, gqkv0:gz0], w_in[:, :, gab0:mg0],
        jnp.zeros((depth, d, N_PROJ - int(end)), w_in.dtype)], axis=-1).astype(BF16)
    wa_b, wc_b, wg_b, wo_b = (w.astype(BF16) for w in (w_attn_o, w_conv_o, w_gdn_o, w_out))
    wqt_b = jnp.swapaxes(peer_wq, 1, 2).astype(BF16)
    keys_b = peer_keys.astype(BF16)
    u_b = peer_u.astype(BF16)
    v_b = peer_v.astype(BF16)
    conv_w_p = jnp.pad(conv_dw_w, ((0, 0), (0, 32 - CONV_W), (0, 0)))
    rope_tabs = _rope_tables(lat_seq)
    cache_k = cache_attn_k.reshape(b_lat, depth, past, A_HEADS * LANES)
    cache_v = cache_attn_v.reshape(b_lat, depth, past, A_HEADS * LANES)

    cond8 = jnp.concatenate([c_ctx[None, :], c, jnp.zeros((SUBLANES - 1 - b_lat, d), F32)], axis=0)
    mod_all = _mod_call(cond8, w_mod, b_mod)

    x = jnp.concatenate([x_prompt.reshape(n_ctx_tok, d), x_sample.reshape(b_lat * lat_seq, d)], axis=0)
    caches, s_list = None, []
    for l in range(depth):
        mod_l = mod_all[l]
        lam_init = 0.8 - 0.6 * math.exp(-0.3 * l)
        proj = _inproj_call(x, mod_l, w_p[l], n_ctx_tok, lat_seq)

        dl = diff_lambda[l]
        ng_a = diff_norm_g[l].reshape(1, LANES)
        oa_ctx, *caches = _attn_ctx_call(proj, dl, ng_a, b_ctx, seq, lam_init, l, depth, caches)
        oa_lat = _attn_lat_call(proj, cache_k, cache_v, l, rope_tabs, dl, ng_a, n_ctx_tok, b_lat, lat_seq,
                                lam_init)

        cvec = lambda a: a[l].reshape(1, C_CONV)
        cargs = (conv_w_p[l], cvec(conv_dw_b), cvec(conv_ln_g), cvec(conv_ln_b))
        oc_ctx = _conv_call(proj, *cargs, 0, b_ctx, seq)
        oc_lat = _conv_call(proj, *cargs, n_ctx_tok // lat_seq, b_lat, lat_seq)

        gargs = (gdn_conv_w[l], _lane_vec(gdn_A_log[l]), _lane_vec(gdn_dt_bias[l]),
                 gdn_norm_g[l].reshape(1, LANES))
        og_ctx, s_new = _gdn_call(proj, *gargs, None, 0, b_ctx, seq, GDN_HEADS_CTX, True)
        og_lat, _ = _gdn_call(proj, *gargs, state_gdn[:, l], n_ctx_tok // lat_seq, b_lat, lat_seq,
                               GDN_HEADS_LAT, False)
        s_list.append(s_new)

        branches = (oa_ctx, oa_lat, oc_ctx, oc_lat, og_ctx, og_lat)
        x1, h2 = _merge_call(x, proj, branches, wa_b[l], wc_b[l], wg_b[l], wo_b[l], mod_l,
                             ln_g[l, 0:1], ln_b[l, 0:1], n_ctx_tok, lat_seq, alpha)
        x = _peer_call(h2, x1, wqt_b[l], keys_b[l], u_b[l], v_b[l], mod_l, ln_g[l, 1:2], ln_b[l, 1:2],
                       n_ctx_tok, lat_seq, alpha)

    y_prompt = x[:n_ctx_tok].reshape(b_ctx, seq, d)
    y_sample = x[n_ctx_tok:].reshape(b_lat, lat_seq, d)
    new_k = caches[0].reshape(b_ctx, depth, seq, A_HEADS, 2 * A_DQK)
    new_v = caches[1].reshape(b_ctx, depth, seq, A_HEADS, A_DV)
    return (y_prompt, y_sample, new_k, new_v, jnp.stack(s_list, axis=1))
```
